```python
import math
import jax, jax.numpy as jnp
from jax import lax
import numpy as np

D_MODEL = 2048
BATCH = 2
SEQ = 4096
DEPTH = 4

MEM_LEN = 256
HEAD_DIM = 128
NSA_HEADS = D_MODEL // 256
NSA_KV_HEADS = 2
NSA_HPG = NSA_HEADS // NSA_KV_HEADS
CMP_BLOCK = 32
CMP_STRIDE = 16
SLC_BLOCK = 64
SLC_TOPK = 16
WINDOW = 512
DIFF_HEADS = D_MODEL // 256
DIFF_QK_DIM = 64
DIFF_V_DIM = 128
MLA_HEADS = D_MODEL // 128
MLA_Q_RANK = 512
MLA_KV_RANK = 512
MLA_NOPE = 128
MLA_ROPE = 64
MLA_V = 128
ROPE_THETA = 10000.0
MEM_HEADS = 4
MEM_HEAD_DIM = 128
FF_DIM = 256 * ((8 * D_MODEL // 3 + 255) // 256)
N_EXPERTS = 8
TOP_K = 2
NUM_BUCKETS = 32
MAX_DISTANCE = 128

Q_BLOCK = 128
RMS_EPS = 1e-6
NEG_INF = -1e30
BIG = 1e30

N_EVEN = (DEPTH + 1) // 2
N_ODD = DEPTH // 2
NSA_Q_W = NSA_HEADS * HEAD_DIM
NSA_KV_W = NSA_KV_HEADS * HEAD_DIM
DIFF_QK_W = DIFF_HEADS * 2 * DIFF_QK_DIM
DIFF_V_W = DIFF_HEADS * DIFF_V_DIM
EVEN_SPLITS = (NSA_Q_W,) + (NSA_KV_W,) * 6 + (3 * NSA_HEADS, DIFF_QK_W, DIFF_QK_W, DIFF_V_W)
EVEN_IN = sum(EVEN_SPLITS)
EVEN_OUT = NSA_Q_W + DIFF_V_W
ODD_IN = MLA_Q_RANK + MLA_KV_RANK + MLA_ROPE

kernel_name = "hybrid_nsa_diff_mla_moe_trunk"


def _split_points(sizes):
    pts, acc = [], 0
    for s in sizes[:-1]:
        acc += s
        pts.append(acc)
    return pts


def rmsnorm(x, g):
    xf = x.astype(jnp.float32)
    y = xf * lax.rsqrt(jnp.mean(xf * xf, axis=-1, keepdims=True) + RMS_EPS)
    return (y * g.astype(jnp.float32)).astype(x.dtype)


def masked_softmax(s, mask):
    s = jnp.where(mask, s.astype(jnp.float32), NEG_INF)
    m = jnp.max(s, axis=-1, keepdims=True)
    p = jnp.where(mask, jnp.exp(s - m), 0.0)
    return p / jnp.maximum(jnp.sum(p, axis=-1, keepdims=True), 1e-30)


def t5_bucket(dist):
    n = jnp.maximum(dist, 0)
    max_exact = NUM_BUCKETS // 2
    nf = jnp.maximum(n, 1).astype(jnp.float32)
    large = max_exact + (jnp.log(nf / max_exact) / math.log(MAX_DISTANCE / max_exact)
                         * (NUM_BUCKETS - max_exact)).astype(jnp.int32)
    large = jnp.minimum(large, NUM_BUCKETS - 1)
    return jnp.where(n < max_exact, n, large)


def unblock(y):
    y = jnp.moveaxis(y, 0, 1)
    return y.reshape((y.shape[0], y.shape[1] * y.shape[2]) + y.shape[3:])


def rope(x, cos, sin):
    half = x.shape[-1] // 2
    x1, x2 = x[..., :half], x[..., half:]
    return jnp.concatenate([x1 * cos - x2 * sin, x2 * cos + x1 * sin], axis=-1).astype(x.dtype)


def nsa_attention(q, kv, gate_logits, cmp_pos, cmp_w1, cmp_w2, bias_tbl):
    B, T, _ = q.shape
    G, HPG, DH = NSA_KV_HEADS, NSA_HPG, HEAD_DIM
    scale = DH ** -0.5
    q = q.reshape(B, T, G, HPG, DH).transpose(0, 2, 3, 1, 4)
    k_cmp, v_cmp, k_slc, v_slc, k_win, v_win = [
        a.reshape(B, T, G, DH).transpose(0, 2, 1, 3) for a in kv]
    tbl = bias_tbl.reshape(NUM_BUCKETS, G, HPG)
    t_all = jnp.arange(T)

    n_cmp = (T - CMP_BLOCK) // CMP_STRIDE + 1
    blk_start = jnp.arange(n_cmp) * CMP_STRIDE
    idx = blk_start[:, None] + jnp.arange(CMP_BLOCK)[None, :]

    def compress(a, pos, w1, w2):
        blocks = a[:, :, idx] + pos
        hdn = jax.nn.gelu(jnp.einsum("bgnf,fe->bgne",
                                     blocks.reshape(B, G, n_cmp, CMP_BLOCK * DH), w1))
        return jnp.einsum("bgne,ed->bgnd", hdn, w2)

    kc = compress(k_cmp, cmp_pos[0], cmp_w1[0], cmp_w2[0])
    vc = compress(v_cmp, cmp_pos[1], cmp_w1[1], cmp_w2[1])
    blk_end = blk_start + CMP_BLOCK - 1
    cmask = blk_end[None, :] <= t_all[:, None]
    cbias = tbl[t5_bucket(t_all[:, None] - blk_end[None, :])].transpose(2, 3, 0, 1)
    s_c = jnp.einsum("bghtd,bgnd->bghtn", q, kc).astype(jnp.float32) * scale + cbias
    p_cmp = masked_softmax(s_c, cmask)
    o_cmp = jnp.einsum("bghtn,bgnd->bghtd", p_cmp.astype(vc.dtype), vc)

    n_slc = T // SLC_BLOCK
    topk = min(SLC_TOPK, n_slc)
    j = jnp.arange(n_slc)
    overlap = ((blk_start[:, None] < (j[None, :] + 1) * SLC_BLOCK)
               & (blk_start[:, None] + CMP_BLOCK > j[None, :] * SLC_BLOCK)).astype(jnp.float32)
    imp = jnp.einsum("bghtn,ns->bgts", p_cmp, overlap)
    cur = t_all[:, None] // SLC_BLOCK
    forced = (j[None, :] == 0) | (j[None, :] == cur) | (j[None, :] == cur - 1)
    visible = j[None, :] * SLC_BLOCK <= t_all[:, None]
    score = jnp.where(forced, BIG, jnp.where(visible, imp, NEG_INF))
    _, sel = lax.top_k(score, topk)

    k_slc_b = k_slc.reshape(B, G, n_slc, SLC_BLOCK, DH)
    v_slc_b = v_slc.reshape(B, G, n_slc, SLC_BLOCK, DH)
    k_win_p = jnp.pad(k_win, ((0, 0), (0, 0), (WINDOW, 0), (0, 0)))
    v_win_p = jnp.pad(v_win, ((0, 0), (0, 0), (WINDOW, 0), (0, 0)))
    gather = jax.vmap(jax.vmap(lambda a, ix: a[ix]))
    g_ix = jnp.arange(G)[None, :, None, None, None]
    off_l = jnp.arange(SLC_BLOCK)
    off_w = jnp.arange(WINDOW + Q_BLOCK)
    kl = topk * SLC_BLOCK

    def to_bqhd(o):
        return o.transpose(0, 3, 1, 2, 4).reshape(B, Q_BLOCK, NSA_HEADS, DH)

    def block(b):
        s0 = b * Q_BLOCK
        tq = s0 + jnp.arange(Q_BLOCK)
        qb = lax.dynamic_slice_in_dim(q, s0, Q_BLOCK, axis=3)
        selb = lax.dynamic_slice_in_dim(sel, s0, Q_BLOCK, axis=2)
        kg = gather(k_slc_b, selb)
        vg = gather(v_slc_b, selb)
        tok = selb[..., None] * SLC_BLOCK + off_l
        tq5 = tq[None, None, :, None, None]
        smask = (tok <= tq5)[:, :, None].reshape(B, G, 1, Q_BLOCK, kl)
        sbias = tbl[t5_bucket(tq5 - tok), g_ix].transpose(0, 1, 5, 2, 3, 4)
        ss = jnp.einsum("bghqd,bgqkld->bghqkl", qb, kg).astype(jnp.float32) * scale + sbias
        ps = masked_softmax(ss.reshape(B, G, HPG, Q_BLOCK, kl), smask).reshape(ss.shape)
        o_s = jnp.einsum("bghqkl,bgqkld->bghqd", ps.astype(vg.dtype), vg)
        kw = lax.dynamic_slice_in_dim(k_win_p, s0, WINDOW + Q_BLOCK, axis=2)
        vw = lax.dynamic_slice_in_dim(v_win_p, s0, WINDOW + Q_BLOCK, axis=2)
        kpos = s0 - WINDOW + off_w
        wdist = tq[:, None] - kpos[None, :]
        wmask = (kpos[None, :] >= 0) & (wdist >= 0) & (wdist < WINDOW)
        wbias = tbl[t5_bucket(wdist)].transpose(2, 3, 0, 1)
        sw = jnp.einsum("bghqd,bgkd->bghqk", qb, kw).astype(jnp.float32) * scale + wbias
        pw = masked_softmax(sw, wmask)
        o_w = jnp.einsum("bghqk,bgkd->bghqd", pw.astype(vw.dtype), vw)
        return to_bqhd(o_s), to_bqhd(o_w)

    o_slc, o_win = lax.map(block, jnp.arange(T // Q_BLOCK))
    o_slc, o_win = unblock(o_slc), unblock(o_win)
    o_cmp = o_cmp.transpose(0, 3, 1, 2, 4).reshape(B, T, NSA_HEADS, DH)
    g = jax.nn.sigmoid(gate_logits.astype(jnp.float32)).reshape(B, T, NSA_HEADS, 3).astype(q.dtype)
    o = g[..., 0:1] * o_cmp + g[..., 1:2] * o_slc + g[..., 2:3] * o_win
    return o.reshape(B, T, NSA_HEADS * DH)


def diff_attention(q, k, v, lam, lam_init, subln_g, bias_tbl):
    B, T, _ = q.shape
    H = DIFF_HEADS
    q = q.reshape(B, T, H, 2, DIFF_QK_DIM)
    k = k.reshape(B, T, H, 2, DIFF_QK_DIM)
    v = v.reshape(B, T, H, DIFF_V_DIM)
    scale = DIFF_QK_DIM ** -0.5
    t_k = jnp.arange(T)

    def block(b):
        s0 = b * Q_BLOCK
        tq = s0 + jnp.arange(Q_BLOCK)
        qb = lax.dynamic_slice_in_dim(q, s0, Q_BLOCK, axis=1)
        dist = tq[:, None] - t_k[None, :]
        bias = bias_tbl[t5_bucket(dist)].transpose(2, 0, 1)[None, :, None]
        s = jnp.einsum("bqhcd,bkhcd->bhcqk", qb, k).astype(jnp.float32) * scale + bias
        p = masked_softmax(s, dist >= 0)
        a = p[:, :, 0] - lam * p[:, :, 1]
        return jnp.einsum("bhqk,bkhd->bqhd", a.astype(v.dtype), v)

    o = unblock(lax.map(block, jnp.arange(T // Q_BLOCK)))
    o = rmsnorm(o, subln_g) * (1.0 - lam_init)
    return o.reshape(B, T, H * DIFF_V_DIM)


def mla_attention(xn, w_in, q_norm_g, kv_norm_g, w_q_up, w_kv_up, cos, sin):
    B, T, _ = xn.shape
    H = MLA_HEADS
    c = xn @ w_in
    cq, ckv, k_pe = jnp.split(c, [MLA_Q_RANK, MLA_Q_RANK + MLA_KV_RANK], axis=-1)
    q = (rmsnorm(cq, q_norm_g) @ w_q_up).reshape(B, T, H, MLA_NOPE + MLA_ROPE)
    q_nope = q[..., :MLA_NOPE]
    q_pe = rope(q[..., MLA_NOPE:], cos[None, :, None], sin[None, :, None])
    kv = (rmsnorm(ckv, kv_norm_g) @ w_kv_up).reshape(B, T, H, MLA_NOPE + MLA_V)
    k_nope, v = kv[..., :MLA_NOPE], kv[..., MLA_NOPE:]
    k_pe = rope(k_pe, cos[None], sin[None])
    scale = (MLA_NOPE + MLA_ROPE) ** -0.5
    t_k = jnp.arange(T)

    def block(b):
        s0 = b * Q_BLOCK
        tq = s0 + jnp.arange(Q_BLOCK)
        qn = lax.dynamic_slice_in_dim(q_nope, s0, Q_BLOCK, axis=1)
        qp = lax.dynamic_slice_in_dim(q_pe, s0, Q_BLOCK, axis=1)
        s = (jnp.einsum("bqhd,bkhd->bhqk", qn, k_nope)
             + jnp.einsum("bqhd,bkd->bhqk", qp, k_pe)).astype(jnp.float32) * scale
        p = masked_softmax(s, tq[:, None] >= t_k[None, :])
        return jnp.einsum("bhqk,bkhd->bqhd", p.astype(v.dtype), v)

    o = unblock(lax.map(block, jnp.arange(T // Q_BLOCK)))
    return o.reshape(B, T, H * MLA_V)


def mem_cross_attention(hn, memn, wq, wkv, wo):
    B, T, _ = hn.shape
    M = memn.shape[1]
    q = (hn @ wq).reshape(B, T, MEM_HEADS, MEM_HEAD_DIM)
    kv = (memn @ wkv).reshape(B, M, 2, MEM_HEADS, MEM_HEAD_DIM)
    k, v = kv[:, :, 0], kv[:, :, 1]
    s = jnp.einsum("bqhd,bkhd->bhqk", q, k).astype(jnp.float32) * MEM_HEAD_DIM ** -0.5
    p = jax.nn.softmax(s, axis=-1)
    o = jnp.einsum("bhqk,bkhd->bqhd", p.astype(v.dtype), v).reshape(B, T, MEM_HEADS * MEM_HEAD_DIM)
    return o @ wo


def swiglu(x, w_gu, w_down):
    g, u = jnp.split(x @ w_gu, 2, axis=-1)
    return (jax.nn.silu(g) * u) @ w_down


def moe_swiglu(x, router_w, router_b, w_gu, w_down):
    B, T, D = x.shape
    xt = x.reshape(B * T, D)
    logits = xt.astype(jnp.float32) @ router_w.astype(jnp.float32) + router_b.astype(jnp.float32)
    top_v, top_i = lax.top_k(logits, TOP_K)
    wts = jax.nn.softmax(top_v, axis=-1)
    combine = jnp.sum(jax.nn.one_hot(top_i, N_EXPERTS, dtype=jnp.float32) * wts[..., None], axis=1)
    y = jnp.zeros_like(xt)
    for e in range(N_EXPERTS):
        y = y + combine[:, e:e + 1].astype(x.dtype) * swiglu(xt, w_gu[e], w_down[e])
    return y.reshape(B, T, D)


def setup_inputs(seed: int = 0) -> dict:
    key = jax.random.key(seed)
    ks = iter(jax.random.split(key, 40))
    f32 = jnp.float32

    def w(shape, fan_in):
        return jax.random.normal(next(ks), shape, f32) * fan_in ** -0.5

    def gain(shape):
        return 1.0 + 0.02 * jax.random.normal(next(ks), shape, f32)

    def small(shape, s):
        return s * jax.random.normal(next(ks), shape, f32)

    return {
        "x": jax.random.normal(next(ks), (BATCH, SEQ, D_MODEL), f32),
        "mem": jax.random.normal(next(ks), (BATCH, MEM_LEN, D_MODEL), f32),
        "rel_bias": small((NUM_BUCKETS, NSA_HEADS + DIFF_HEADS), 0.5),
        "norm_mix": gain((DEPTH, D_MODEL)),
        "norm_mem": gain((DEPTH, D_MODEL)),
        "norm_ca": gain((DEPTH, D_MODEL)),
        "norm_ffn": gain((DEPTH, D_MODEL)),
        "norm_final": gain((D_MODEL,)),
        "even_w_in": w((N_EVEN, D_MODEL, EVEN_IN), D_MODEL),
        "even_w_out": w((N_EVEN, EVEN_OUT, D_MODEL), EVEN_OUT),
        "nsa_cmp_pos": small((N_EVEN, 2, CMP_BLOCK, HEAD_DIM), 0.1),
        "nsa_cmp_w1": w((N_EVEN, 2, CMP_BLOCK * HEAD_DIM, HEAD_DIM), CMP_BLOCK * HEAD_DIM),
        "nsa_cmp_w2": w((N_EVEN, 2, HEAD_DIM, HEAD_DIM), HEAD_DIM),
        "diff_lambda": small((N_EVEN, 4, DIFF_QK_DIM), 0.1),
        "diff_subln": gain((N_EVEN, DIFF_V_DIM)),
        "ffn_w_gu": w((N_EVEN, D_MODEL, 2 * FF_DIM), D_MODEL),
        "ffn_w_down": w((N_EVEN, FF_DIM, D_MODEL), FF_DIM),
        "odd_w_in": w((N_ODD, D_MODEL, ODD_IN), D_MODEL),
        "mla_q_norm": gain((N_ODD, MLA_Q_RANK)),
        "mla_kv_norm": gain((N_ODD, MLA_KV_RANK)),
        "mla_w_q_up": w((N_ODD, MLA_Q_RANK, MLA_HEADS * (MLA_NOPE + MLA_ROPE)), MLA_Q_RANK),
        "mla_w_kv_up": w((N_ODD, MLA_KV_RANK, MLA_HEADS * (MLA_NOPE + MLA_V)), MLA_KV_RANK),
        "odd_w_out": w((N_ODD, MLA_HEADS * MLA_V, D_MODEL), MLA_HEADS * MLA_V),
        "router_w": w((N_ODD, D_MODEL, N_EXPERTS), D_MODEL),
        "router_b": small((N_ODD, N_EXPERTS), 0.01),
        "moe_w_gu": w((N_ODD, N_EXPERTS, D_MODEL, 2 * FF_DIM), D_MODEL),
        "moe_w_down": w((N_ODD, N_EXPERTS, FF_DIM, D_MODEL), FF_DIM),
        "ca_wq": w((DEPTH, D_MODEL, MEM_HEADS * MEM_HEAD_DIM), D_MODEL),
        "ca_wkv": w((DEPTH, D_MODEL, 2 * MEM_HEADS * MEM_HEAD_DIM), D_MODEL),
        "ca_wo": w((DEPTH, MEM_HEADS * MEM_HEAD_DIM, D_MODEL), MEM_HEADS * MEM_HEAD_DIM),
    }


def reference(x, mem, rel_bias, norm_mix, norm_mem, norm_ca, norm_ffn, norm_final,
              even_w_in, even_w_out, nsa_cmp_pos, nsa_cmp_w1, nsa_cmp_w2, diff_lambda, diff_subln,
              ffn_w_gu, ffn_w_down,
              odd_w_in, mla_q_norm, mla_kv_norm, mla_w_q_up, mla_w_kv_up, odd_w_out,
              router_w, router_b, moe_w_gu, moe_w_down,
              ca_wq, ca_wkv, ca_wo):
    T = x.shape[1]
    inv_freq = ROPE_THETA ** (-jnp.arange(0, MLA_ROPE, 2, dtype=jnp.float32) / MLA_ROPE)
    ang = jnp.arange(T, dtype=jnp.float32)[:, None] * inv_freq
    cos, sin = jnp.cos(ang), jnp.sin(ang)
    bias_a, bias_b = rel_bias[:, :NSA_HEADS], rel_bias[:, NSA_HEADS:]
    split_pts = _split_points(EVEN_SPLITS)
    h = x
    for i in range(DEPTH):
        li = i // 2
        xn = rmsnorm(h, norm_mix[i])
        if i % 2 == 0:
            parts = jnp.split(xn @ even_w_in[li], split_pts, axis=-1)
            lq1, lk1, lq2, lk2 = diff_lambda[li].astype(jnp.float32)
            lam_init = 0.8 - 0.6 * math.exp(-0.3 * i)
            lam = jnp.exp(jnp.sum(lq1 * lk1)) - jnp.exp(jnp.sum(lq2 * lk2)) + lam_init
            o_a = nsa_attention(parts[0], parts[1:7], parts[7], nsa_cmp_pos[li],
                                nsa_cmp_w1[li], nsa_cmp_w2[li], bias_a)
            o_b = diff_attention(parts[8], parts[9], parts[10], lam, lam_init, diff_subln[li], bias_b)
            mix = jnp.concatenate([o_a, o_b], axis=-1) @ even_w_out[li]
        else:
            mix = mla_attention(xn, odd_w_in[li], mla_q_norm[li], mla_kv_norm[li],
                                mla_w_q_up[li], mla_w_kv_up[li], cos, sin) @ odd_w_out[li]
        h = h + mix
        h = h + mem_cross_attention(rmsnorm(h, norm_ca[i]), rmsnorm(mem, norm_mem[i]),
                                    ca_wq[i], ca_wkv[i], ca_wo[i])
        hn = rmsnorm(h, norm_ffn[i])
        if i % 2 == 0:
            h = h + swiglu(hn, ffn_w_gu[li], ffn_w_down[li])
        else:
            h = h + moe_swiglu(hn, router_w[li], router_b[li], moe_w_gu[li], moe_w_down[li])
    return rmsnorm(h, norm_final)
```

```python
import functools
import math

import jax
import jax.numpy as jnp
from jax import lax
from jax.experimental import pallas as pl
from jax.experimental.pallas import tpu as pltpu

F32 = jnp.float32
BF16 = jnp.bfloat16
I32 = jnp.int32

HEAD_DIM = 128
NSA_HEADS = 8
NSA_KV_HEADS = 2
NSA_HPG = NSA_HEADS // NSA_KV_HEADS
CMP_BLOCK = 32
CMP_STRIDE = 16
SLC_BLOCK = 64
SLC_TOPK = 16
WINDOW = 512
DIFF_HEADS = 8
DIFF_QK_DIM = 64
MLA_HEADS = 16
MLA_Q_RANK = 512
MLA_KV_RANK = 512
MLA_NOPE = 128
MLA_ROPE = 64
MLA_V = 128
ROPE_THETA = 10000.0
MEM_HEADS = 4
MEM_HEAD_DIM = 128
N_EXPERTS = 8
NUM_BUCKETS = 32
MAX_DISTANCE = 128
RMS_EPS = 1e-6
NEG_INF = -1e30
BIG = 1e30

LANES = 128
VMEM_LIMIT = 56 * 1024 * 1024
ATT_C = 256
MLA_C = 512
EXPERT_TM = 512
HIGHEST = lax.Precision.HIGHEST


def _cparams(n_axes):
    return pltpu.CompilerParams(dimension_semantics=("arbitrary",) * n_axes,
                                vmem_limit_bytes=VMEM_LIMIT)


def _dot(a, b):
    return jnp.dot(a, b, preferred_element_type=F32)


def _dot_nt(a, b):
    return lax.dot_general(a, b, (((1,), (1,)), ((), ())), preferred_element_type=F32)


def _rms_kernel(x_ref, g_ref, o_ref):
    x = x_ref[...].astype(F32)
    y = x * lax.rsqrt(jnp.mean(x * x, axis=-1, keepdims=True) + RMS_EPS)
    o_ref[...] = (y * g_ref[...]).astype(o_ref.dtype)


def rmsnorm(x, g, out_dtype, width=None, col_block=0, tm=512):
    m = x.shape[0]
    width = width or x.shape[1]
    tm = min(tm, m)
    return pl.pallas_call(
        _rms_kernel,
        grid=(m // tm,),
        in_specs=[pl.BlockSpec((tm, width), lambda i: (i, col_block)),
                  pl.BlockSpec((1, width), lambda i: (0, 0))],
        out_specs=pl.BlockSpec((tm, width), lambda i: (i, 0)),
        out_shape=jax.ShapeDtypeStruct((m, width), out_dtype),
        compiler_params=_cparams(1),
        name="rmsnorm",
    )(x, g.reshape(1, width).astype(F32))


def _gmm_kernel(te_ref, x_ref, w_ref, *rest, swiglu, has_res):
    del te_ref
    o_ref = rest[-1]
    x = x_ref[...].astype(BF16)
    acc = _dot(x, w_ref[...])
    if swiglu:
        up = _dot(x, rest[0][...])
        acc = acc * jax.nn.sigmoid(acc) * up
    if has_res:
        acc = acc + rest[-2][...]
    o_ref[...] = acc.astype(o_ref.dtype)


def gmm(x, w, tile_e, *, tm, tn, out_dtype, swiglu=False, res=None):
    r, k = x.shape
    n = w.shape[2] // 2 if swiglu else w.shape[2]
    nj = n // tn
    in_specs = [pl.BlockSpec((tm, k), lambda j, i, te: (i, 0)),
                pl.BlockSpec((None, k, tn), lambda j, i, te: (te[i], 0, j))]
    args = [x, w]
    if swiglu:
        in_specs.append(pl.BlockSpec((None, k, tn), lambda j, i, te: (te[i], 0, j + nj)))
        args.append(w)
    if res is not None:
        in_specs.append(pl.BlockSpec((tm, tn), lambda j, i, te: (i, j)))
        args.append(res)
    return pl.pallas_call(
        functools.partial(_gmm_kernel, swiglu=swiglu, has_res=res is not None),
        grid_spec=pltpu.PrefetchScalarGridSpec(
            num_scalar_prefetch=1,
            grid=(nj, r // tm),
            in_specs=in_specs,
            out_specs=pl.BlockSpec((tm, tn), lambda j, i, te: (i, j))),
        out_shape=jax.ShapeDtypeStruct((r, n), out_dtype),
        compiler_params=_cparams(2),
        name="gmm_swiglu" if swiglu else "gmm",
    )(tile_e, *args)


def mm(x, w, *, out_dtype, tm=512, tn=512, swiglu=False, res=None):
    m = x.shape[0]
    tm = min(tm, m)
    n = w.shape[1] // 2 if swiglu else w.shape[1]
    tn = min(tn, n)
    return gmm(x, w[None], jnp.zeros((m // tm,), I32), tm=tm, tn=tn,
               out_dtype=out_dtype, swiglu=swiglu, res=res)


def _t5_bucket(dist):
    n = jnp.maximum(dist, 0)
    max_exact = NUM_BUCKETS // 2
    nf = jnp.maximum(n, 1).astype(F32)
    large = max_exact + (jnp.log(nf / max_exact) / math.log(MAX_DISTANCE / max_exact)
                         * (NUM_BUCKETS - max_exact)).astype(I32)
    large = jnp.minimum(large, NUM_BUCKETS - 1)
    return jnp.where(n < max_exact, n, large)


def _bias_lookup(dist, tbl):
    oh = jax.nn.one_hot(_t5_bucket(dist), NUM_BUCKETS, dtype=F32)
    return jnp.einsum("...k,kh->h...", oh, tbl.astype(F32), precision=HIGHEST)


def _bias_tiles(tbl, c):
    i = jnp.arange(c)[:, None]
    j = jnp.arange(c)[None, :]
    tiles = jnp.stack([_bias_lookup(i - j, tbl), _bias_lookup(c + i - j, tbl)], axis=1)
    far = _bias_lookup(jnp.asarray(2 * c), tbl)
    assert c + 1 >= MAX_DISTANCE
    return tiles, far


def _compress_kernel(a_ref, w1_ref, w2_ref, pos_ref, o_ref):
    nblk = a_ref.shape[0]
    pq = _dot(a_ref[...], w1_ref[...])
    r = _dot(pos_ref[...], w1_ref[...])
    c = r[0:1, :HEAD_DIM] + r[1:2, HEAD_DIM:]
    hp = pq[:, :HEAD_DIM] + pltpu.roll(pq[:, HEAD_DIM:], nblk - 1, 0) + c
    hdn = jax.nn.gelu(hp)
    o_ref[...] = _dot(hdn.astype(BF16), w2_ref[...]).astype(o_ref.dtype)


def nsa_compress(acat, w1cat, w2, pos2):
    _, b, g, nblk, wid = acat.shape
    return pl.pallas_call(
        _compress_kernel,
        grid=(2, b, g),
        in_specs=[pl.BlockSpec((None, None, None, nblk, wid), lambda s, bi, gi: (s, bi, gi, 0, 0)),
                  pl.BlockSpec((None, wid, 2 * HEAD_DIM), lambda s, bi, gi: (s, 0, 0)),
                  pl.BlockSpec((None, HEAD_DIM, HEAD_DIM), lambda s, bi, gi: (s, 0, 0)),
                  pl.BlockSpec((None, 8, wid), lambda s, bi, gi: (s, 0, 0))],
        out_specs=pl.BlockSpec((None, None, None, nblk, HEAD_DIM), lambda s, bi, gi: (s, bi, gi, 0, 0)),
        out_shape=jax.ShapeDtypeStruct((2, b, g, nblk, HEAD_DIM), BF16),
        compiler_params=_cparams(3),
        name="nsa_compress",
    )(acat, w1cat, w2, pos2)


def _nsa_cmp_kernel(q_ref, kc_ref, vc_ref, cb_ref, ov_ref, o_ref, sel_ref, *, tq, topk):
    qi = pl.program_id(2)
    ncb = kc_ref.shape[0]
    nslc = sel_ref.shape[1]
    t = qi * tq + lax.broadcasted_iota(I32, (tq, ncb), 0)
    n = lax.broadcasted_iota(I32, (tq, ncb), 1)
    mask = n * CMP_STRIDE + (CMP_BLOCK - 1) <= t
    kc = kc_ref[...]
    vc = vc_ref[...]
    psum = jnp.zeros((tq, ncb), F32)
    for h in range(NSA_HPG):
        qh = q_ref[:, h * HEAD_DIM:(h + 1) * HEAD_DIM]
        s = jnp.where(mask, _dot_nt(qh, kc) + cb_ref[h], NEG_INF)
        m = jnp.max(s, axis=-1, keepdims=True)
        p = jnp.where(mask, jnp.exp(s - m), 0.0)
        p = p / jnp.maximum(jnp.sum(p, axis=-1, keepdims=True), 1e-30)
        o_ref[:, h * HEAD_DIM:(h + 1) * HEAD_DIM] = _dot(p.astype(BF16), vc).astype(o_ref.dtype)
        psum = psum + p
    imp = jnp.dot(psum, ov_ref[...], precision=HIGHEST, preferred_element_type=F32)
    tt = qi * tq + lax.broadcasted_iota(I32, (tq, nslc), 0)
    j = lax.broadcasted_iota(I32, (tq, nslc), 1)
    cur = tt // SLC_BLOCK
    forced = (j == 0) | (j == cur) | (j == cur - 1)
    visible = j * SLC_BLOCK <= tt
    score = jnp.where(forced, BIG, jnp.where(visible, imp, NEG_INF))
    rank = jnp.zeros((tq, nslc), F32)
    for i in range(nslc):
        col = score[:, i:i + 1]
        tie = jnp.where(j > i, 1.0, 0.0)
        rank = rank + jnp.where(col > score, 1.0, jnp.where(col == score, tie, 0.0))
    sel_ref[...] = jnp.where(rank < topk, 1.0, 0.0).astype(sel_ref.dtype)


def nsa_cmp_select(p3, kvc, cbias, overlap, *, tq=ATT_C):
    b, t, _ = p3.shape
    g = NSA_KV_HEADS
    ncb = kvc.shape[3]
    nslc = t // SLC_BLOCK
    gw = NSA_HPG * HEAD_DIM
    return pl.pallas_call(
        functools.partial(_nsa_cmp_kernel, tq=tq, topk=min(SLC_TOPK, nslc)),
        grid=(b, g, t // tq),
        in_specs=[pl.BlockSpec((None, tq, gw), lambda bi, gi, qi: (bi, qi, gi)),
                  pl.BlockSpec((None, None, None, ncb, HEAD_DIM), lambda bi, gi, qi: (0, bi, gi, 0, 0)),
                  pl.BlockSpec((None, None, None, ncb, HEAD_DIM), lambda bi, gi, qi: (1, bi, gi, 0, 0)),
                  pl.BlockSpec((NSA_HPG, tq, ncb), lambda bi, gi, qi: (gi, qi, 0)),
                  pl.BlockSpec((ncb, nslc), lambda bi, gi, qi: (0, 0))],
        out_specs=[pl.BlockSpec((None, tq, gw), lambda bi, gi, qi: (bi, qi, gi)),
                   pl.BlockSpec((None, None, tq, nslc), lambda bi, gi, qi: (bi, gi, qi, 0))],
        out_shape=[jax.ShapeDtypeStruct((b, t, NSA_HEADS * HEAD_DIM), F32),
                   jax.ShapeDtypeStruct((b, g, t, nslc), BF16)],
        compiler_params=_cparams(3),
        name="nsa_cmp_select",
    )(p3, kvc, kvc, cbias, overlap)


def _attn_kernel(*refs, cfg):
    kind = cfg["kind"]
    c = cfg["c"]
    ns = cfg["ns"]
    has_bias = kind != "mla"
    it = iter(refs)
    far_ref = next(it) if has_bias else None
    lam_ref = next(it) if kind == "diff" else None
    q_ref, k_ref, v_ref = next(it), next(it), next(it)
    bt_ref = next(it) if has_bias else None
    if kind == "slc":
        sel_ref, e_ref = next(it), next(it)
    if kind == "win":
        gl_ref, ocmp_ref, oslc_ref = next(it), next(it), next(it)
    if kind == "diff":
        g2_ref = next(it)
    o_ref = next(it)
    qs_ref = next(it) if ns > 1 else None
    m_ref, l_ref, acc_ref = next(it), next(it), next(it)

    hg = pl.program_id(1)
    qi = pl.program_id(2)
    dv = acc_ref.shape[1]

    if kind in ("slc", "win"):
        for h in range(ns):
            qs_ref[h * c:(h + 1) * c, :] = q_ref[:, h * HEAD_DIM:(h + 1) * HEAD_DIM]
    elif kind == "diff":
        q = q_ref[...]
        lane = lax.broadcasted_iota(I32, q.shape, 1)
        qs_ref[0:c, :] = jnp.where(lane < DIFF_QK_DIM, q, jnp.zeros_like(q))
        qs_ref[c:2 * c, :] = jnp.where(lane >= DIFF_QK_DIM, q, jnp.zeros_like(q))
    m_ref[...] = jnp.full(m_ref.shape, NEG_INF, F32)
    l_ref[...] = jnp.zeros(l_ref.shape, F32)
    acc_ref[...] = jnp.zeros(acc_ref.shape, F32)

    row = lax.broadcasted_iota(I32, (c, c), 0)
    col = lax.broadcasted_iota(I32, (c, c), 1)

    def step(ci, mode):
        k0 = pl.multiple_of(ci * c, c)
        kc = k_ref[pl.ds(k0, c), :]
        vc = v_ref[pl.ds(k0, c), :]
        qall = qs_ref[...] if ns > 1 else q_ref[...]
        s_all = _dot_nt(qall, kc)
        mask = None
        if kind == "slc":
            selm = _dot(sel_ref[...], e_ref[:, pl.ds(k0, c)])
            if mode == "diag":
                selm = jnp.where(row >= col, selm, 0.0)
            mask = selm > 0.5
        elif mode == "diag":
            mask = row >= col
        elif mode == "edge":
            mask = col > row
        for h in range(ns):
            r0, r1 = h * c, (h + 1) * c
            s = s_all[r0:r1]
            if has_bias:
                hb = hg * ns + h if kind in ("slc", "win") else hg
                if mode in ("far", "edge"):
                    s = s + far_ref[hb]
                else:
                    bt = bt_ref[h if kind in ("slc", "win") else 0, 0 if mode == "diag" else 1]
                    s = s + bt
            if mask is not None:
                s = jnp.where(mask, s, NEG_INF)
            m_prev = m_ref[r0:r1]
            m_new = jnp.maximum(m_prev, jnp.max(s, axis=-1, keepdims=True))
            alpha = jnp.exp(m_prev - m_new)
            p = jnp.exp(s - m_new)
            if mask is not None:
                p = jnp.where(mask, p, 0.0)
            l_ref[r0:r1] = alpha * l_ref[r0:r1] + jnp.sum(p, axis=-1, keepdims=True)
            acc_ref[r0:r1] = alpha * acc_ref[r0:r1] + _dot(p.astype(BF16), vc)
            m_ref[r0:r1] = m_new

    if kind == "win":
        nback = WINDOW // c
        assert nback * c == WINDOW and nback >= 1
        for back in range(nback, 0, -1):
            mode = "edge" if back == nback else ("near" if back == 1 else "far")

            @pl.when(qi - back >= 0)
            def _(back=back, mode=mode):
                step(qi - back, mode)
        if nback == 1:
            raise NotImplementedError("window chunking needs c < WINDOW")
        step(qi, "diag")
    else:
        def far_body(ci, carry):
            step(ci, "far")
            return carry
        n_far = jnp.maximum(qi - 1, 0) if has_bias else qi
        lax.fori_loop(0, n_far, far_body, 0)
        if has_bias:
            @pl.when(qi >= 1)
            def _():
                step(qi - 1, "near")
        step(qi, "diag")

    def out_stream(h):
        r0, r1 = h * c, (h + 1) * c
        return acc_ref[r0:r1] / jnp.maximum(l_ref[r0:r1], 1e-30)

    if kind == "slc":
        for h in range(ns):
            o_ref[:, h * dv:(h + 1) * dv] = out_stream(h).astype(o_ref.dtype)
    elif kind == "win":
        gate = jax.nn.sigmoid(gl_ref[...])
        for h in range(ns):
            sl = slice(h * dv, (h + 1) * dv)
            o = (gate[:, 3 * h:3 * h + 1] * ocmp_ref[:, sl]
                 + gate[:, 3 * h + 1:3 * h + 2] * oslc_ref[:, sl]
                 + gate[:, 3 * h + 2:3 * h + 3] * out_stream(h))
            o_ref[:, sl] = o.astype(o_ref.dtype)
    elif kind == "diff":
        o = out_stream(0) - lam_ref[0] * out_stream(1)
        y = o * lax.rsqrt(jnp.mean(o * o, axis=-1, keepdims=True) + RMS_EPS)
        o_ref[...] = (y * g2_ref[...]).astype(o_ref.dtype)
    else:
        o_ref[...] = out_stream(0).astype(o_ref.dtype)


def _attn_call(kind, scalars, tensors, in_specs, out_spec, out_shape, grid, *, c, ns, dk, dv):
    scratch = []
    if ns > 1:
        scratch.append(pltpu.VMEM((ns * c, dk), BF16))
    scratch += [pltpu.VMEM((ns * c, 1), F32), pltpu.VMEM((ns * c, 1), F32), pltpu.VMEM((ns * c, dv), F32)]
    cfg = dict(kind=kind, c=c, ns=ns)
    return pl.pallas_call(
        functools.partial(_attn_kernel, cfg=cfg),
        grid_spec=pltpu.PrefetchScalarGridSpec(
            num_scalar_prefetch=len(scalars), grid=grid, in_specs=in_specs,
            out_specs=out_spec, scratch_shapes=scratch),
        out_shape=out_shape,
        compiler_params=_cparams(3),
        name="attn_" + kind,
    )(*scalars, *tensors)


def nsa_slc(p3, sel, expand, tiles, far, *, c=ATT_C):
    b, t, _ = p3.shape
    g, hpg, dh = NSA_KV_HEADS, NSA_HPG, HEAD_DIM
    nslc = t // SLC_BLOCK
    kb, vb = 12, 14
    in_specs = [pl.BlockSpec((None, c, hpg * dh), lambda bi, gi, qi, *_: (bi, qi, gi)),
                pl.BlockSpec((None, t, dh), lambda bi, gi, qi, *_: (bi, 0, kb + gi)),
                pl.BlockSpec((None, t, dh), lambda bi, gi, qi, *_: (bi, 0, vb + gi)),
                pl.BlockSpec((hpg, 2, c, c), lambda bi, gi, qi, *_: (gi, 0, 0, 0)),
                pl.BlockSpec((None, None, c, nslc), lambda bi, gi, qi, *_: (bi, gi, qi, 0)),
                pl.BlockSpec((nslc, t), lambda bi, gi, qi, *_: (0, 0))]
    return _attn_call(
        "slc", [far], [p3, p3, p3, tiles, sel, expand], in_specs,
        pl.BlockSpec((None, c, hpg * dh), lambda bi, gi, qi, *_: (bi, qi, gi)),
        jax.ShapeDtypeStruct((b, t, NSA_HEADS * dh), F32), (b, g, t // c),
        c=c, ns=hpg, dk=dh, dv=dh)


def nsa_win(p3, gates3, o_cmp, o_slc, tiles, far, *, c=ATT_C):
    b, t, _ = p3.shape
    g, hpg, dh = NSA_KV_HEADS, NSA_HPG, HEAD_DIM
    kb, vb = 16, 18
    ospec = pl.BlockSpec((None, c, hpg * dh), lambda bi, gi, qi, *_: (bi, qi, gi))
    in_specs = [ospec,
                pl.BlockSpec((None, t, dh), lambda bi, gi, qi, *_: (bi, 0, kb + gi)),
                pl.BlockSpec((None, t, dh), lambda bi, gi, qi, *_: (bi, 0, vb + gi)),
                pl.BlockSpec((hpg, 2, c, c), lambda bi, gi, qi, *_: (gi, 0, 0, 0)),
                pl.BlockSpec((None, c, LANES), lambda bi, gi, qi, *_: (bi, qi, gi)),
                ospec, ospec]
    return _attn_call(
        "win", [far], [p3, p3, p3, tiles, gates3, o_cmp, o_slc], in_specs, ospec,
        jax.ShapeDtypeStruct((b, t, NSA_HEADS * dh), BF16), (b, g, t // c),
        c=c, ns=hpg, dk=dh, dv=dh)


def diff_attn(p3, tiles, far, lam, g2, *, c=ATT_C):
    b, t, _ = p3.shape
    dh = HEAD_DIM
    qb, kb, vb = 20, 28, 36
    in_specs = [pl.BlockSpec((None, c, dh), lambda bi, hi, qi, *_: (bi, qi, qb + hi)),
                pl.BlockSpec((None, t, dh), lambda bi, hi, qi, *_: (bi, 0, kb + hi)),
                pl.BlockSpec((None, t, dh), lambda bi, hi, qi, *_: (bi, 0, vb + hi)),
                pl.BlockSpec((1, 2, c, c), lambda bi, hi, qi, *_: (hi, 0, 0, 0)),
                pl.BlockSpec((1, dh), lambda bi, hi, qi, *_: (0, 0))]
    return _attn_call(
        "diff", [far, lam], [p3, p3, p3, tiles, g2], in_specs,
        pl.BlockSpec((None, c, dh), lambda bi, hi, qi, *_: (bi, qi, hi)),
        jax.ShapeDtypeStruct((b, t, DIFF_HEADS * dh), BF16), (b, DIFF_HEADS, t // c),
        c=c, ns=2, dk=dh, dv=dh)


def mla_attn(qf, kf, kv3, *, c=MLA_C):
    b, t, _ = qf.shape
    dk = 2 * HEAD_DIM
    c = min(c, t)
    in_specs = [pl.BlockSpec((None, c, dk), lambda bi, hi, qi: (bi, qi, hi)),
                pl.BlockSpec((None, t, dk), lambda bi, hi, qi: (bi, 0, hi)),
                pl.BlockSpec((None, t, MLA_V), lambda bi, hi, qi: (bi, 0, MLA_HEADS + hi))]
    return _attn_call(
        "mla", [], [qf, kf, kv3], in_specs,
        pl.BlockSpec((None, c, MLA_V), lambda bi, hi, qi: (bi, qi, hi)),
        jax.ShapeDtypeStruct((b, t, MLA_HEADS * MLA_V), BF16), (b, MLA_HEADS, t // c),
        c=c, ns=1, dk=dk, dv=MLA_V)


def _mla_assemble_kernel(q_ref, kv_ref, c_ref, cos_ref, sin_ref, qf_ref, kf_ref):
    h16 = MLA_HEADS
    cos = cos_ref[...]
    sin = sin_ref[...]
    lane = lax.broadcasted_iota(I32, cos.shape, 1)
    first = lane < MLA_ROPE
    kpe = c_ref[:, 0:LANES] * cos + c_ref[:, LANES:2 * LANES] * sin
    for j in range(h16 // 2):
        pe = q_ref[:, (h16 + j) * LANES:(h16 + j + 1) * LANES]
        pep = q_ref[:, (h16 + h16 // 2 + j) * LANES:(h16 + h16 // 2 + j + 1) * LANES]
        qpe = pe * cos + pep * sin
        for par in range(2):
            h = 2 * j + par
            keep = first if par == 0 else jnp.logical_not(first)
            qf_ref[:, (2 * h) * LANES:(2 * h + 1) * LANES] = q_ref[:, h * LANES:(h + 1) * LANES].astype(BF16)
            qf_ref[:, (2 * h + 1) * LANES:(2 * h + 2) * LANES] = jnp.where(keep, qpe, 0.0).astype(BF16)
            kf_ref[:, (2 * h) * LANES:(2 * h + 1) * LANES] = kv_ref[:, h * LANES:(h + 1) * LANES].astype(BF16)
            kf_ref[:, (2 * h + 1) * LANES:(2 * h + 2) * LANES] = jnp.where(keep, kpe, 0.0).astype(BF16)


def mla_assemble(qraw, kvraw, craw, cos2, sin2, seq, *, tm=256):
    n = qraw.shape[0]
    tm = min(tm, seq)
    nt = seq // tm
    wq = qraw.shape[1]
    return pl.pallas_call(
        _mla_assemble_kernel,
        grid=(n // tm,),
        in_specs=[pl.BlockSpec((tm, wq), lambda i: (i, 0)),
                  pl.BlockSpec((tm, MLA_HEADS * LANES), lambda i: (i, 0)),
                  pl.BlockSpec((tm, 2 * LANES), lambda i: (i, 4)),
                  pl.BlockSpec((tm, LANES), lambda i: (i % nt, 0)),
                  pl.BlockSpec((tm, LANES), lambda i: (i % nt, 0))],
        out_specs=[pl.BlockSpec((tm, 2 * MLA_HEADS * LANES), lambda i: (i, 0)),
                   pl.BlockSpec((tm, 2 * MLA_HEADS * LANES), lambda i: (i, 0))],
        out_shape=[jax.ShapeDtypeStruct((n, 2 * MLA_HEADS * LANES), BF16)] * 2,
        compiler_params=_cparams(1),
        name="mla_assemble",
    )(qraw, kvraw, craw, cos2, sin2)


def _mem_attn_kernel(q_ref, kv_ref, o_ref):
    hd = MEM_HEAD_DIM
    voff = MEM_HEADS * hd
    for h in range(MEM_HEADS):
        sl = slice(h * hd, (h + 1) * hd)
        s = _dot_nt(q_ref[:, sl], kv_ref[:, sl])
        m = jnp.max(s, axis=-1, keepdims=True)
        p = jnp.exp(s - m)
        p = p / jnp.sum(p, axis=-1, keepdims=True)
        o_ref[:, sl] = _dot(p.astype(BF16), kv_ref[:, voff + h * hd:voff + (h + 1) * hd]).astype(o_ref.dtype)


def mem_attn(q3, kv3, *, tq=512):
    b, t, w = q3.shape
    m = kv3.shape[1]
    tq = min(tq, t)
    return pl.pallas_call(
        _mem_attn_kernel,
        grid=(b, t // tq),
        in_specs=[pl.BlockSpec((None, tq, w), lambda bi, qi: (bi, qi, 0)),
                  pl.BlockSpec((None, m, 2 * w), lambda bi, qi: (bi, 0, 0))],
        out_specs=pl.BlockSpec((None, tq, w), lambda bi, qi: (bi, qi, 0)),
        out_shape=jax.ShapeDtypeStruct((b, t, w), BF16),
        compiler_params=_cparams(2),
        name="mem_attn",
    )(q3, kv3)


def _router_kernel(h_ref, g_ref, rw_ref, rb_ref, hn_ref, ro_ref):
    x = h_ref[...]
    hn = x * lax.rsqrt(jnp.mean(x * x, axis=-1, keepdims=True) + RMS_EPS) * g_ref[...]
    hn_ref[...] = hn
    logits = jnp.dot(hn, rw_ref[...], precision=HIGHEST, preferred_element_type=F32) + rb_ref[...]
    lane = lax.broadcasted_iota(I32, logits.shape, 1)
    lg = jnp.where(lane < N_EXPERTS, logits, -jnp.inf)
    v1 = jnp.max(lg, axis=-1, keepdims=True)
    i1 = jnp.min(jnp.where(lg == v1, lane, LANES), axis=-1, keepdims=True)
    lg2 = jnp.where(lane == i1, -jnp.inf, lg)
    v2 = jnp.max(lg2, axis=-1, keepdims=True)
    i2 = jnp.min(jnp.where(lg2 == v2, lane, LANES), axis=-1, keepdims=True)
    e2 = jnp.exp(v2 - v1)
    w1 = 1.0 / (1.0 + e2)
    w2 = e2 / (1.0 + e2)
    ro_ref[...] = jnp.where(lane == 0, i1.astype(F32),
                            jnp.where(lane == 1, i2.astype(F32),
                                      jnp.where(lane == 2, w1, jnp.where(lane == 3, w2, 0.0))))


def moe_router(h, g, rw, rb, *, tm=256):
    n, d = h.shape
    rwp = jnp.zeros((d, LANES), F32).at[:, :N_EXPERTS].set(rw.astype(F32))
    rbp = jnp.zeros((1, LANES), F32).at[0, :N_EXPERTS].set(rb.astype(F32))
    return pl.pallas_call(
        _router_kernel,
        grid=(n // tm,),
        in_specs=[pl.BlockSpec((tm, d), lambda i: (i, 0)),
                  pl.BlockSpec((1, d), lambda i: (0, 0)),
                  pl.BlockSpec((d, LANES), lambda i: (0, 0)),
                  pl.BlockSpec((1, LANES), lambda i: (0, 0))],
        out_specs=[pl.BlockSpec((tm, d), lambda i: (i, 0)),
                   pl.BlockSpec((tm, LANES), lambda i: (i, 0))],
        out_shape=[jax.ShapeDtypeStruct((n, d), F32), jax.ShapeDtypeStruct((n, LANES), F32)],
        compiler_params=_cparams(1),
        name="moe_router",
    )(h, g.reshape(1, d).astype(F32), rwp, rbp)


def _row_copy(src, dst, sem):
    return pltpu.make_async_copy(src, dst, sem)


def _dispatch_kernel(d1_ref, d2_ref, x_ref, xs_in_ref, xs_ref, sem):
    del xs_in_ref
    tm = x_ref.shape[0]
    base = pl.program_id(0) * tm

    def issue(r, carry):
        src = x_ref.at[pl.ds(r, 1), :]
        _row_copy(src, xs_ref.at[pl.ds(d1_ref[base + r], 1), :], sem).start()
        _row_copy(src, xs_ref.at[pl.ds(d2_ref[base + r], 1), :], sem).start()
        return carry

    def drain(r, carry):
        src = x_ref.at[pl.ds(r, 1), :]
        _row_copy(src, xs_ref.at[pl.ds(0, 1), :], sem).wait()
        _row_copy(src, xs_ref.at[pl.ds(0, 1), :], sem).wait()
        return carry

    lax.fori_loop(0, tm, issue, 0)
    lax.fori_loop(0, tm, drain, 0)


def moe_dispatch(hn, d1, d2, rows, *, tm=256):
    n, d = hn.shape
    return pl.pallas_call(
        _dispatch_kernel,
        grid_spec=pltpu.PrefetchScalarGridSpec(
            num_scalar_prefetch=2, grid=(n // tm,),
            in_specs=[pl.BlockSpec((tm, d), lambda i, *_: (i, 0)),
                      pl.BlockSpec(memory_space=pl.ANY)],
            out_specs=pl.BlockSpec(memory_space=pl.ANY),
            scratch_shapes=[pltpu.SemaphoreType.DMA(())]),
        out_shape=jax.ShapeDtypeStruct((rows, d), hn.dtype),
        input_output_aliases={3: 0},
        compiler_params=_cparams(1),
        name="moe_dispatch",
    )(d1, d2, hn, jnp.zeros((rows, d), hn.dtype))


def _combine_kernel(d1_ref, d2_ref, ys_ref, h_ref, ro_ref, o_ref, b1_ref, b2_ref, sem):
    tm = h_ref.shape[0]
    base = pl.program_id(0) * tm

    def issue(r, carry):
        _row_copy(ys_ref.at[pl.ds(d1_ref[base + r], 1), :], b1_ref.at[pl.ds(r, 1), :], sem).start()
        _row_copy(ys_ref.at[pl.ds(d2_ref[base + r], 1), :], b2_ref.at[pl.ds(r, 1), :], sem).start()
        return carry

    def drain(r, carry):
        _row_copy(ys_ref.at[pl.ds(0, 1), :], b1_ref.at[pl.ds(r, 1), :], sem).wait()
        _row_copy(ys_ref.at[pl.ds(0, 1), :], b2_ref.at[pl.ds(r, 1), :], sem).wait()
        return carry

    lax.fori_loop(0, tm, issue, 0)
    lax.fori_loop(0, tm, drain, 0)
    ro = ro_ref[...]
    o_ref[...] = h_ref[...] + ro[:, 2:3] * b1_ref[...] + ro[:, 3:4] * b2_ref[...]


def moe_combine(ys, h, ro, d1, d2, *, tm=256):
    n, d = h.shape
    return pl.pallas_call(
        _combine_kernel,
        grid_spec=pltpu.PrefetchScalarGridSpec(
            num_scalar_prefetch=2, grid=(n // tm,),
            in_specs=[pl.BlockSpec(memory_space=pl.ANY),
                      pl.BlockSpec((tm, d), lambda i, *_: (i, 0)),
                      pl.BlockSpec((tm, LANES), lambda i, *_: (i, 0))],
            out_specs=pl.BlockSpec((tm, d), lambda i, *_: (i, 0)),
            scratch_shapes=[pltpu.VMEM((tm, d), F32), pltpu.VMEM((tm, d), F32),
                            pltpu.SemaphoreType.DMA(())]),
        out_shape=jax.ShapeDtypeStruct((n, d), F32),
        compiler_params=_cparams(1),
        name="moe_combine",
    )(d1, d2, ys, h, ro)


def moe_block(h, g, rw, rb, w_gu, w_down, *, tm_e=EXPERT_TM):
    n, d = h.shape
    tm_e = min(tm_e, n)
    hn, ro = moe_router(h, g, rw, rb)
    e1 = ro[:, 0].astype(I32)
    e2 = ro[:, 1].astype(I32)
    ind = (jax.nn.one_hot(e1, N_EXPERTS, dtype=I32) + jax.nn.one_hot(e2, N_EXPERTS, dtype=I32))
    csum = jnp.cumsum(ind, axis=0)
    rank = csum - ind
    cnt = csum[-1]
    pcnt = (cnt + tm_e - 1) // tm_e * tm_e
    ends = jnp.cumsum(pcnt)
    off = ends - pcnt
    d1 = (off[e1] + jnp.take_along_axis(rank, e1[:, None], axis=1)[:, 0]).astype(I32)
    d2 = (off[e2] + jnp.take_along_axis(rank, e2[:, None], axis=1)[:, 0]).astype(I32)
    rows = 2 * n + N_EXPERTS * tm_e
    tile_start = jnp.arange(rows // tm_e, dtype=I32) * tm_e
    tile_e = jnp.minimum(jnp.sum((tile_start[:, None] >= ends[None, :]).astype(I32), axis=1),
                         N_EXPERTS - 1).astype(I32)
    xs = moe_dispatch(hn, d1, d2, rows)
    act = gmm(xs, w_gu, tile_e, tm=tm_e, tn=512, out_dtype=BF16, swiglu=True)
    ys = gmm(act, w_down, tile_e, tm=tm_e, tn=512, out_dtype=F32)
    return moe_combine(ys, h, ro, d1, d2)


def _mem_block(h, mem2, b, t, g_ca, g_mem, wq, wkv, wo):
    n, d = h.shape
    hn = rmsnorm(h, g_ca, BF16)
    memn = rmsnorm(mem2, g_mem, BF16)
    q = mm(hn, (wq * MEM_HEAD_DIM ** -0.5).astype(BF16), out_dtype=BF16)
    kv = mm(memn, wkv.astype(BF16), out_dtype=BF16)
    o = mem_attn(q.reshape(b, t, -1), kv.reshape(b, mem2.shape[0] // b, -1))
    return mm(o.reshape(n, -1), wo.astype(BF16), out_dtype=F32, res=h)


def _even_layer(h, b, t, li, lam_init, p, tabs):
    n, d = h.shape
    w_in = p["even_w_in"][li]
    nq, nkv = NSA_HEADS * HEAD_DIM, NSA_KV_HEADS * HEAD_DIM
    g0 = nq + 6 * nkv
    dq0 = g0 + 3 * NSA_HEADS
    dqw = DIFF_HEADS * 2 * DIFF_QK_DIM
    w_main = jnp.concatenate([w_in[:, :nq] * HEAD_DIM ** -0.5, w_in[:, nq:g0],
                              w_in[:, dq0:dq0 + dqw] * DIFF_QK_DIM ** -0.5, w_in[:, dq0 + dqw:]],
                             axis=1).astype(BF16)
    w_gate = jnp.zeros((d, NSA_KV_HEADS * LANES), F32)
    for g in range(NSA_KV_HEADS):
        w_gate = w_gate.at[:, g * LANES:g * LANES + 3 * NSA_HPG].set(
            w_in[:, g0 + g * 3 * NSA_HPG:g0 + (g + 1) * 3 * NSA_HPG])
    xn = rmsnorm(h, p["norm_mix"][2 * li], BF16)
    proj = mm(xn, w_main, out_dtype=BF16)
    gates = mm(xn, w_gate.astype(BF16), out_dtype=F32, tn=256)
    p3 = proj.reshape(b, t, -1)

    kvs = p3[:, :, nq:nq + 2 * nkv].reshape(b, t, 2, NSA_KV_HEADS, HEAD_DIM)
    acat = kvs.transpose(2, 0, 3, 1, 4).reshape(2, b, NSA_KV_HEADS, t // CMP_STRIDE, CMP_STRIDE * HEAD_DIM)
    w1 = p["nsa_cmp_w1"][li]
    half = CMP_STRIDE * HEAD_DIM
    w1cat = jnp.concatenate([w1[:, :half], w1[:, half:]], axis=2).astype(BF16)
    pos = p["nsa_cmp_pos"][li].reshape(2, 2, half)
    pos2 = jnp.zeros((2, 8, half), F32).at[:, :2].set(pos).astype(BF16)
    kvc = nsa_compress(acat, w1cat, p["nsa_cmp_w2"][li].astype(BF16), pos2)
    o_cmp, sel = nsa_cmp_select(p3, kvc, tabs["cbias"], tabs["overlap"])
    o_slc = nsa_slc(p3, sel, tabs["expand"], tabs["tiles_a"], tabs["far_a"])
    o_a = nsa_win(p3, gates.reshape(b, t, -1), o_cmp, o_slc, tabs["tiles_a"], tabs["far_a"])

    lq1, lk1, lq2, lk2 = p["diff_lambda"][li].astype(F32)
    lam = jnp.exp(jnp.sum(lq1 * lk1)) - jnp.exp(jnp.sum(lq2 * lk2)) + lam_init
    g2 = (p["diff_subln"][li].astype(F32) * (1.0 - lam_init)).reshape(1, HEAD_DIM)
    o_b = diff_attn(p3, tabs["tiles_b"], tabs["far_b"], lam.reshape(1), g2)

    mix_in = jnp.concatenate([o_a, o_b], axis=-1).reshape(n, -1)
    h = mm(mix_in, p["even_w_out"][li].astype(BF16), out_dtype=F32, res=h)
    return h


def _odd_layer(h, b, t, li, p, tabs):
    n, d = h.shape
    hh, nope, rp = MLA_HEADS, MLA_NOPE, MLA_ROPE
    scale = (nope + rp) ** -0.5

    def rot(w):
        w = w.reshape(w.shape[0], -1, 2, rp // 2)
        return jnp.stack([-w[:, :, 1], w[:, :, 0]], axis=2).reshape(w.shape[0], -1)

    w_in = p["odd_w_in"][li]
    kpe_w = w_in[:, MLA_Q_RANK + MLA_KV_RANK:]
    w_in_ext = jnp.concatenate([w_in[:, :MLA_Q_RANK + MLA_KV_RANK], kpe_w, kpe_w, rot(kpe_w), rot(kpe_w)],
                               axis=1).astype(BF16)
    wq = (p["mla_w_q_up"][li] * scale).reshape(MLA_Q_RANK, hh, nope + rp)
    wq_pe = wq[:, :, nope:].reshape(MLA_Q_RANK, hh * rp)
    wq_ext = jnp.concatenate([wq[:, :, :nope].reshape(MLA_Q_RANK, hh * nope), wq_pe, rot(wq_pe)],
                             axis=1).astype(BF16)
    wkv = p["mla_w_kv_up"][li].reshape(MLA_KV_RANK, hh, nope + MLA_V)
    wkv_ext = jnp.concatenate([wkv[:, :, :nope].reshape(MLA_KV_RANK, -1),
                               wkv[:, :, nope:].reshape(MLA_KV_RANK, -1)], axis=1).astype(BF16)

    xn = rmsnorm(h, p["norm_mix"][2 * li + 1], BF16)
    craw = mm(xn, w_in_ext, out_dtype=F32, tn=256)
    cqn = rmsnorm(craw, p["mla_q_norm"][li], BF16, width=MLA_Q_RANK, col_block=0)
    ckvn = rmsnorm(craw, p["mla_kv_norm"][li], BF16, width=MLA_KV_RANK, col_block=1)
    qraw = mm(cqn, wq_ext, out_dtype=F32)
    kvraw = mm(ckvn, wkv_ext, out_dtype=BF16)
    qf, kf = mla_assemble(qraw, kvraw, craw, tabs["cos2"], tabs["sin2"], t)
    o = mla_attn(qf.reshape(b, t, -1), kf.reshape(b, t, -1), kvraw.reshape(b, t, -1))
    return mm(o.reshape(n, -1), p["odd_w_out"][li].astype(BF16), out_dtype=F32, res=h)


def _tables(rel_bias, t):
    bias_a, bias_b = rel_bias[:, :NSA_HEADS], rel_bias[:, NSA_HEADS:]
    tiles_a, far_a = _bias_tiles(bias_a, ATT_C)
    tiles_b, far_b = _bias_tiles(bias_b, ATT_C)
    ncb = t // CMP_STRIDE
    nslc = t // SLC_BLOCK
    tt = jnp.arange(t)
    blk_start = jnp.arange(ncb) * CMP_STRIDE
    cbias = _bias_lookup(tt[:, None] - (blk_start + CMP_BLOCK - 1)[None, :], bias_a)
    jj = jnp.arange(nslc)
    n_cmp = (t - CMP_BLOCK) // CMP_STRIDE + 1
    overlap = ((blk_start[:, None] < (jj[None, :] + 1) * SLC_BLOCK)
               & (blk_start[:, None] + CMP_BLOCK > jj[None, :] * SLC_BLOCK)
               & (jnp.arange(ncb)[:, None] < n_cmp)).astype(F32)
    expand = (jj[:, None] == (tt[None, :] // SLC_BLOCK)).astype(BF16)
    inv_freq = ROPE_THETA ** (-jnp.arange(0, MLA_ROPE, 2, dtype=F32) / MLA_ROPE)
    ang = jnp.arange(t, dtype=F32)[:, None] * inv_freq
    cos2 = jnp.tile(jnp.cos(ang), (1, 2 * LANES // MLA_ROPE))
    sin2 = jnp.tile(jnp.sin(ang), (1, 2 * LANES // MLA_ROPE))
    return dict(tiles_a=tiles_a, far_a=far_a, tiles_b=tiles_b, far_b=far_b, cbias=cbias,
                overlap=overlap, expand=expand, cos2=cos2, sin2=sin2)


def kernel(x, mem, rel_bias, norm_mix, norm_mem, norm_ca, norm_ffn, norm_final, even_w_in, even_w_out, nsa_cmp_pos, nsa_cmp_w1, nsa_cmp_w2, diff_lambda, diff_subln, ffn_w_gu, ffn_w_down, odd_w_in, mla_q_norm, mla_kv_norm, mla_w_q_up, mla_w_kv_up, odd_w_out, router_w, router_b, moe_w_gu, moe_w_down, ca_wq, ca_wkv, ca_wo):
    p = dict(norm_mix=norm_mix, even_w_in=even_w_in, even_w_out=even_w_out, nsa_cmp_pos=nsa_cmp_pos,
             nsa_cmp_w1=nsa_cmp_w1, nsa_cmp_w2=nsa_cmp_w2, diff_lambda=diff_lambda, diff_subln=diff_subln,
             odd_w_in=odd_w_in, mla_q_norm=mla_q_norm, mla_kv_norm=mla_kv_norm, mla_w_q_up=mla_w_q_up,
             mla_w_kv_up=mla_w_kv_up, odd_w_out=odd_w_out)
    b, t, d = x.shape
    depth = norm_mix.shape[0]
    tabs = _tables(rel_bias, t)
    h = x.reshape(b * t, d).astype(F32)
    mem2 = mem.reshape(-1, d).astype(F32)
    for i in range(depth):
        li = i // 2
        if i % 2 == 0:
            h = _even_layer(h, b, t, li, 0.8 - 0.6 * math.exp(-0.3 * i), p, tabs)
        else:
            h = _odd_layer(h, b, t, li, p, tabs)
        h = _mem_block(h, mem2, b, t, norm_ca[i], norm_mem[i], ca_wq[i], ca_wkv[i], ca_wo[i])
        if i % 2 == 0:
            hn = rmsnorm(h, norm_ffn[i], BF16)
            act = mm(hn, ffn_w_gu[li].astype(BF16), out_dtype=BF16, swiglu=True)
            h = mm(act, ffn_w_down[li].astype(BF16), out_dtype=F32, res=h)
        else:
            h = moe_block(h, norm_ffn[i], router_w[li], router_b[li],
                          moe_w_gu[li].astype(BF16), moe_w_down[li].astype(BF16))
    return rmsnorm(h, norm_final, x.dtype).reshape(b, t, d)
```

```python
import functools
import math

import jax
import jax.numpy as jnp
from jax import lax
from jax.experimental import pallas as pl
from jax.experimental.pallas import tpu as pltpu

F32 = jnp.float32
BF16 = jnp.bfloat16
I32 = jnp.int32

HEAD_DIM = 128
NSA_HEADS = 8
NSA_KV_HEADS = 2
NSA_HPG = NSA_HEADS // NSA_KV_HEADS
CMP_BLOCK = 32
CMP_STRIDE = 16
SLC_BLOCK = 64
SLC_TOPK = 16
WINDOW = 512
DIFF_HEADS = 8
DIFF_QK_DIM = 64
MLA_HEADS = 16
MLA_Q_RANK = 512
MLA_KV_RANK = 512
MLA_NOPE = 128
MLA_ROPE = 64
MLA_V = 128
ROPE_THETA = 10000.0
MEM_HEADS = 4
MEM_HEAD_DIM = 128
N_EXPERTS = 8
NUM_BUCKETS = 32
MAX_DISTANCE = 128
RMS_EPS = 1e-6
NEG_INF = -1e30
BIG = 1e30

LANES = 128
VMEM_LIMIT = 56 * 1024 * 1024
ATT_C = 256
MLA_C = 512
EXPERT_TM = 512
HIGHEST = lax.Precision.HIGHEST


def _cparams(n_axes):
    return pltpu.CompilerParams(dimension_semantics=("arbitrary",) * n_axes,
                                vmem_limit_bytes=VMEM_LIMIT)


def _dot(a, b):
    return jnp.dot(a, b, preferred_element_type=F32)


def _rep(x, n):
    return x if n == 1 else jnp.concatenate([x] * n, axis=1)


def _rep_cat(xs):
    return xs[0] if len(xs) == 1 else jnp.concatenate(xs, axis=1)


def _dot_nt(a, b):
    return lax.dot_general(a, b, (((1,), (1,)), ((), ())), preferred_element_type=F32)


def _rms_kernel(x_ref, g_ref, o_ref):
    x = x_ref[...].astype(F32)
    y = x * lax.rsqrt(jnp.mean(x * x, axis=-1, keepdims=True) + RMS_EPS)
    o_ref[...] = (y * g_ref[...]).astype(o_ref.dtype)


def rmsnorm(x, g, out_dtype, width=None, col_block=0, tm=512):
    m = x.shape[0]
    width = width or x.shape[1]
    tm = min(tm, m)
    return pl.pallas_call(
        _rms_kernel,
        grid=(m // tm,),
        in_specs=[pl.BlockSpec((tm, width), lambda i: (i, col_block)),
                  pl.BlockSpec((1, width), lambda i: (0, 0))],
        out_specs=pl.BlockSpec((tm, width), lambda i: (i, 0)),
        out_shape=jax.ShapeDtypeStruct((m, width), out_dtype),
        compiler_params=_cparams(1),
        name="rmsnorm",
    )(x, g.reshape(1, width).astype(F32))


def _gmm_kernel(te_ref, x_ref, w_ref, *rest, swiglu, has_res, cast_w):
    nw = 2 if swiglu else 1
    w_refs = (w_ref,) + tuple(rest[:nw - 1])
    res_ref = rest[nw - 1] if has_res else None
    o_ref = rest[nw - 1 + int(has_res)]
    if cast_w:
        wb_refs = rest[nw + int(has_res):]
        i = pl.program_id(1)
        prev = te_ref[jnp.maximum(i - 1, 0)]

        @pl.when((i == 0) | (te_ref[i] != prev))
        def _():
            for src, dst in zip(w_refs, wb_refs):
                dst[...] = src[...].astype(BF16)
        w_refs = wb_refs
    x = x_ref[...].astype(BF16)
    acc = _dot(x, w_refs[0][...])
    if swiglu:
        acc = acc * jax.nn.sigmoid(acc) * _dot(x, w_refs[1][...])
    if has_res:
        acc = acc + res_ref[...]
    o_ref[...] = acc.astype(o_ref.dtype)


def gmm(x, w, tile_e, *, tm, tn, out_dtype, swiglu=False, res=None):
    r, k = x.shape
    n = w.shape[2] // 2 if swiglu else w.shape[2]
    nj = n // tn
    cast_w = w.dtype != BF16
    in_specs = [pl.BlockSpec((tm, k), lambda j, i, te: (i, 0)),
                pl.BlockSpec((None, k, tn), lambda j, i, te: (te[i], 0, j))]
    args = [x, w]
    if swiglu:
        in_specs.append(pl.BlockSpec((None, k, tn), lambda j, i, te: (te[i], 0, j + nj)))
        args.append(w)
    if res is not None:
        in_specs.append(pl.BlockSpec((tm, tn), lambda j, i, te: (i, j)))
        args.append(res)
    scratch = [pltpu.VMEM((k, tn), BF16)] * ((2 if swiglu else 1) if cast_w else 0)
    return pl.pallas_call(
        functools.partial(_gmm_kernel, swiglu=swiglu, has_res=res is not None, cast_w=cast_w),
        grid_spec=pltpu.PrefetchScalarGridSpec(
            num_scalar_prefetch=1,
            grid=(nj, r // tm),
            in_specs=in_specs,
            out_specs=pl.BlockSpec((tm, tn), lambda j, i, te: (i, j)),
            scratch_shapes=scratch),
        out_shape=jax.ShapeDtypeStruct((r, n), out_dtype),
        compiler_params=_cparams(2),
        name="gmm_swiglu" if swiglu else "gmm",
    )(tile_e, *args)


def mm(x, w, *, out_dtype, tm=512, tn=512, swiglu=False, res=None, group=0):
    m = x.shape[0]
    tm = min(tm, m)
    if w.ndim == 2:
        w = w[None]
    n = w.shape[2] // 2 if swiglu else w.shape[2]
    tn = min(tn, n)
    return gmm(x, w, jnp.full((m // tm,), group, I32), tm=tm, tn=tn,
               out_dtype=out_dtype, swiglu=swiglu, res=res)


def _t5_bucket(dist):
    n = jnp.maximum(dist, 0)
    max_exact = NUM_BUCKETS // 2
    nf = jnp.maximum(n, 1).astype(F32)
    large = max_exact + (jnp.log(nf / max_exact) / math.log(MAX_DISTANCE / max_exact)
                         * (NUM_BUCKETS - max_exact)).astype(I32)
    large = jnp.minimum(large, NUM_BUCKETS - 1)
    return jnp.where(n < max_exact, n, large)


def _bias_lookup(dist, tbl):
    oh = jax.nn.one_hot(_t5_bucket(dist), NUM_BUCKETS, dtype=F32)
    return jnp.einsum("...k,kh->h...", oh, tbl.astype(F32), precision=HIGHEST)


def _bias_tiles(tbl, c):
    i = jnp.arange(c)[:, None]
    j = jnp.arange(c)[None, :]
    tiles = jnp.stack([_bias_lookup(i - j, tbl), _bias_lookup(c + i - j, tbl)], axis=1)
    far = _bias_lookup(jnp.asarray(2 * c), tbl)
    assert c + 1 >= MAX_DISTANCE
    return tiles, far


def _compress_kernel(a_ref, w1_ref, w2_ref, pos_ref, o_ref):
    nblk = a_ref.shape[0]
    pq = _dot(a_ref[...], w1_ref[...])
    r = _dot(pos_ref[...], w1_ref[...])
    c = r[0:1, :HEAD_DIM] + r[1:2, HEAD_DIM:]
    hp = pq[:, :HEAD_DIM] + pltpu.roll(pq[:, HEAD_DIM:], nblk - 1, 0) + c
    hdn = jax.nn.gelu(hp)
    o_ref[...] = _dot(hdn.astype(BF16), w2_ref[...]).astype(o_ref.dtype)


def nsa_compress(acat, w1cat, w2, pos2):
    _, b, g, nblk, wid = acat.shape
    return pl.pallas_call(
        _compress_kernel,
        grid=(2, b, g),
        in_specs=[pl.BlockSpec((None, None, None, nblk, wid), lambda s, bi, gi: (s, bi, gi, 0, 0)),
                  pl.BlockSpec((None, wid, 2 * HEAD_DIM), lambda s, bi, gi: (s, 0, 0)),
                  pl.BlockSpec((None, HEAD_DIM, HEAD_DIM), lambda s, bi, gi: (s, 0, 0)),
                  pl.BlockSpec((None, 8, wid), lambda s, bi, gi: (s, 0, 0))],
        out_specs=pl.BlockSpec((None, None, None, nblk, HEAD_DIM), lambda s, bi, gi: (s, bi, gi, 0, 0)),
        out_shape=jax.ShapeDtypeStruct((2, b, g, nblk, HEAD_DIM), BF16),
        compiler_params=_cparams(3),
        name="nsa_compress",
    )(acat, w1cat, w2, pos2)


def _nsa_cmp_kernel(q_ref, kc_ref, vc_ref, cb_ref, ov_ref, o_ref, sel_ref, *, tq, topk):
    qi = pl.program_id(2)
    ncb = kc_ref.shape[0]
    nslc = sel_ref.shape[1]
    t = qi * tq + lax.broadcasted_iota(I32, (tq, ncb), 0)
    n = lax.broadcasted_iota(I32, (tq, ncb), 1)
    mask = n * CMP_STRIDE + (CMP_BLOCK - 1) <= t
    kc = kc_ref[...]
    vc = vc_ref[...]
    psum = jnp.zeros((tq, ncb), F32)
    for h in range(NSA_HPG):
        qh = q_ref[:, h * HEAD_DIM:(h + 1) * HEAD_DIM]
        s = jnp.where(mask, _dot_nt(qh, kc) + cb_ref[h], NEG_INF)
        m = jnp.max(s, axis=-1, keepdims=True)
        p = jnp.where(mask, jnp.exp(s - m), 0.0)
        p = p / jnp.maximum(jnp.sum(p, axis=-1, keepdims=True), 1e-30)
        o_ref[:, h * HEAD_DIM:(h + 1) * HEAD_DIM] = _dot(p.astype(BF16), vc).astype(o_ref.dtype)
        psum = psum + p
    imp = jnp.dot(psum, ov_ref[...], precision=HIGHEST, preferred_element_type=F32)
    tt = qi * tq + lax.broadcasted_iota(I32, (tq, nslc), 0)
    j = lax.broadcasted_iota(I32, (tq, nslc), 1)
    cur = tt // SLC_BLOCK
    forced = (j == 0) | (j == cur) | (j == cur - 1)
    visible = j * SLC_BLOCK <= tt
    score = jnp.where(forced, BIG, jnp.where(visible, imp, NEG_INF))
    rank = jnp.zeros((tq, nslc), F32)
    for i in range(nslc):
        col = score[:, i:i + 1]
        tie = jnp.where(j > i, 1.0, 0.0)
        rank = rank + jnp.where(col > score, 1.0, jnp.where(col == score, tie, 0.0))
    sel_ref[...] = jnp.where(rank < topk, 1.0, 0.0).astype(sel_ref.dtype)


def nsa_cmp_select(p3, kvc, cbias, overlap, *, tq=ATT_C):
    b, t, _ = p3.shape
    g = NSA_KV_HEADS
    ncb = kvc.shape[3]
    nslc = t // SLC_BLOCK
    gw = NSA_HPG * HEAD_DIM
    return pl.pallas_call(
        functools.partial(_nsa_cmp_kernel, tq=tq, topk=min(SLC_TOPK, nslc)),
        grid=(b, g, t // tq),
        in_specs=[pl.BlockSpec((None, tq, gw), lambda bi, gi, qi: (bi, qi, gi)),
                  pl.BlockSpec((None, None, None, ncb, HEAD_DIM), lambda bi, gi, qi: (0, bi, gi, 0, 0)),
                  pl.BlockSpec((None, None, None, ncb, HEAD_DIM), lambda bi, gi, qi: (1, bi, gi, 0, 0)),
                  pl.BlockSpec((NSA_HPG, tq, ncb), lambda bi, gi, qi: (gi, qi, 0)),
                  pl.BlockSpec((ncb, nslc), lambda bi, gi, qi: (0, 0))],
        out_specs=[pl.BlockSpec((None, tq, gw), lambda bi, gi, qi: (bi, qi, gi)),
                   pl.BlockSpec((None, None, tq, nslc), lambda bi, gi, qi: (bi, gi, qi, 0))],
        out_shape=[jax.ShapeDtypeStruct((b, t, NSA_HEADS * HEAD_DIM), F32),
                   jax.ShapeDtypeStruct((b, g, t, nslc), BF16)],
        compiler_params=_cparams(3),
        name="nsa_cmp_select",
    )(p3, kvc, kvc, cbias, overlap)


def _attn_kernel(*refs, cfg):
    kind = cfg["kind"]
    c = cfg["c"]
    ns = cfg["ns"]
    has_bias = kind != "mla"
    it = iter(refs)
    far_ref = next(it) if has_bias else None
    lam_ref = next(it) if kind == "diff" else None
    q_ref, k_ref, v_ref = next(it), next(it), next(it)
    bt_ref = next(it) if has_bias else None
    if kind == "slc":
        sel_ref, e_ref = next(it), next(it)
    if kind == "win":
        gl_ref, ocmp_ref, oslc_ref = next(it), next(it), next(it)
    if kind == "diff":
        g2_ref = next(it)
    o_ref = next(it)
    qs_ref = next(it) if kind != "mla" else None
    m_ref, acc_ref = next(it), next(it)

    hg = pl.program_id(1)
    qi = pl.program_id(2)
    dv = acc_ref.shape[1] - LANES

    gr = m_ref.shape[0] // ns

    if kind in ("slc", "win"):
        for h in range(ns):
            qs_ref[h * gr:(h + 1) * gr, :] = q_ref[:, h * HEAD_DIM:(h + 1) * HEAD_DIM]
    elif kind == "diff":
        q = q_ref[...]
        lane = lax.broadcasted_iota(I32, q.shape, 1)
        qs_ref[0:gr, :] = jnp.where(lane < DIFF_QK_DIM, q, jnp.zeros_like(q))
        qs_ref[gr:2 * gr, :] = jnp.where(lane >= DIFF_QK_DIM, q, jnp.zeros_like(q))
    else:
        qs_ref = q_ref
    m_ref[...] = jnp.full(m_ref.shape, NEG_INF, F32)
    acc_ref[...] = jnp.zeros(acc_ref.shape, F32)

    row = lax.broadcasted_iota(I32, (gr, c), 0)
    col = lax.broadcasted_iota(I32, (gr, c), 1)

    def step(k0, w, mode):
        k0 = pl.multiple_of(k0, c)
        kc = k_ref[pl.ds(k0, w), :]
        vc = v_ref[pl.ds(k0, w), :]
        v_ext = jnp.concatenate([vc, jnp.ones((w, LANES), vc.dtype)], axis=1)
        mask = None
        if kind == "slc":
            selm = _dot(sel_ref[...], e_ref[:, pl.ds(k0, w)])
            if mode == "diag":
                selm = jnp.where(row >= col, selm, 0.0)
            mask = selm > 0.5
        elif mode == "diag" and kind != "mla":
            mask = row >= col
        elif mode == "edge":
            mask = col > row
        for h in range(ns):
            r0, r1 = h * gr, (h + 1) * gr
            qh = qs_ref[r0:r1, :]
            sw = min(c, ATT_C)
            nsub = w // sw
            s = _rep_cat([_dot_nt(qh, kc[u * sw:(u + 1) * sw]) for u in range(nsub)])
            shift = None
            if has_bias:
                hb = hg * ns + h if kind in ("slc", "win") else hg
                if mode in ("far", "edge"):
                    shift = far_ref[hb]
                else:
                    s = s + bt_ref[h if kind in ("slc", "win") else 0, 0 if mode == "diag" else 1]
            if kind == "mla" and mode == "diag":
                mask = row + h * gr >= col
            if mask is not None:
                s = jnp.where(mask, s, NEG_INF)
            mx = jnp.max(s, axis=-1, keepdims=True)
            if shift is not None:
                mx = mx + shift
            m_prev = m_ref[r0:r1]
            m_new = jnp.maximum(m_prev, mx)
            alpha = jnp.exp(m_prev - m_new)
            sub = m_new if shift is None else m_new - shift
            p = jnp.exp(s - _rep(sub, w // LANES))
            if mask is not None:
                p = jnp.where(mask, p, 0.0)
            pb = p.astype(BF16)
            pv = _dot(pb[:, :sw], v_ext[:sw])
            for u in range(1, nsub):
                pv = pv + _dot(pb[:, u * sw:(u + 1) * sw], v_ext[u * sw:(u + 1) * sw])
            acc_ref[r0:r1] = _rep(alpha, acc_ref.shape[1] // LANES) * acc_ref[r0:r1] + pv
            m_ref[r0:r1] = m_new

    if kind == "win":
        nback = WINDOW // c
        assert nback * c == WINDOW and nback >= 2
        for back in range(nback, 0, -1):
            mode = "edge" if back == nback else ("near" if back == 1 else "far")

            @pl.when(qi - back >= 0)
            def _(back=back, mode=mode):
                step((qi - back) * c, c, mode)
        step(qi * c, c, "diag")
    elif kind == "mla":
        def far_body(ci, carry):
            step(ci * c, c, "far")
            return carry
        lax.fori_loop(0, qi, far_body, 0)
        step(qi * c, c, "diag")
    else:
        n_far = jnp.maximum(qi - 1, 0)

        def far_body(pi, carry):
            step(pi * (2 * c), 2 * c, "far")
            return carry
        lax.fori_loop(0, n_far // 2, far_body, 0)

        @pl.when(n_far % 2 == 1)
        def _():
            step((n_far - 1) * c, c, "far")

        @pl.when(qi >= 1)
        def _():
            step((qi - 1) * c, c, "near")
        step(qi * c, c, "diag")

    def out_stream(h):
        r0, r1 = h * gr, (h + 1) * gr
        return acc_ref[r0:r1, :dv] / jnp.maximum(acc_ref[r0:r1, dv:], 1e-30)

    if kind == "slc":
        for h in range(ns):
            o_ref[:, h * dv:(h + 1) * dv] = out_stream(h).astype(o_ref.dtype)
    elif kind == "win":
        gate = jax.nn.sigmoid(gl_ref[...])
        for h in range(ns):
            sl = slice(h * dv, (h + 1) * dv)
            o = (gate[:, 3 * h:3 * h + 1] * ocmp_ref[:, sl]
                 + gate[:, 3 * h + 1:3 * h + 2] * oslc_ref[:, sl]
                 + gate[:, 3 * h + 2:3 * h + 3] * out_stream(h))
            o_ref[:, sl] = o.astype(o_ref.dtype)
    elif kind == "diff":
        o = out_stream(0) - lam_ref[0] * out_stream(1)
        y = o * lax.rsqrt(jnp.mean(o * o, axis=-1, keepdims=True) + RMS_EPS)
        o_ref[...] = (y * g2_ref[...]).astype(o_ref.dtype)
    else:
        for h in range(ns):
            o_ref[h * gr:(h + 1) * gr, :] = out_stream(h).astype(o_ref.dtype)


def _attn_call(kind, scalars, tensors, in_specs, out_spec, out_shape, grid, *, c, ns, gr, dk, dv):
    scratch = []
    if kind != "mla":
        scratch.append(pltpu.VMEM((ns * gr, dk), BF16))
    assert dv == LANES
    scratch += [pltpu.VMEM((ns * gr, LANES), F32), pltpu.VMEM((ns * gr, dv + LANES), F32)]
    cfg = dict(kind=kind, c=c, ns=ns)
    return pl.pallas_call(
        functools.partial(_attn_kernel, cfg=cfg),
        grid_spec=pltpu.PrefetchScalarGridSpec(
            num_scalar_prefetch=len(scalars), grid=grid, in_specs=in_specs,
            out_specs=out_spec, scratch_shapes=scratch),
        out_shape=out_shape,
        compiler_params=_cparams(3),
        name="attn_" + kind,
    )(*scalars, *tensors)


def nsa_slc(p3, sel, expand, tiles, far, *, c=ATT_C):
    b, t, _ = p3.shape
    g, hpg, dh = NSA_KV_HEADS, NSA_HPG, HEAD_DIM
    nslc = t // SLC_BLOCK
    kb, vb = 12, 14
    in_specs = [pl.BlockSpec((None, c, hpg * dh), lambda bi, gi, qi, *_: (bi, qi, gi)),
                pl.BlockSpec((None, t, dh), lambda bi, gi, qi, *_: (bi, 0, kb + gi)),
                pl.BlockSpec((None, t, dh), lambda bi, gi, qi, *_: (bi, 0, vb + gi)),
                pl.BlockSpec((hpg, 2, c, c), lambda bi, gi, qi, *_: (gi, 0, 0, 0)),
                pl.BlockSpec((None, None, c, nslc), lambda bi, gi, qi, *_: (bi, gi, qi, 0)),
                pl.BlockSpec((nslc, t), lambda bi, gi, qi, *_: (0, 0))]
    return _attn_call(
        "slc", [far], [p3, p3, p3, tiles, sel, expand], in_specs,
        pl.BlockSpec((None, c, hpg * dh), lambda bi, gi, qi, *_: (bi, qi, gi)),
        jax.ShapeDtypeStruct((b, t, NSA_HEADS * dh), F32), (b, g, t // c),
        c=c, ns=hpg, gr=c, dk=dh, dv=dh)


def nsa_win(p3, gates3, o_cmp, o_slc, tiles, far, *, c=ATT_C):
    b, t, _ = p3.shape
    g, hpg, dh = NSA_KV_HEADS, NSA_HPG, HEAD_DIM
    kb, vb = 16, 18
    ospec = pl.BlockSpec((None, c, hpg * dh), lambda bi, gi, qi, *_: (bi, qi, gi))
    in_specs = [ospec,
                pl.BlockSpec((None, t, dh), lambda bi, gi, qi, *_: (bi, 0, kb + gi)),
                pl.BlockSpec((None, t, dh), lambda bi, gi, qi, *_: (bi, 0, vb + gi)),
                pl.BlockSpec((hpg, 2, c, c), lambda bi, gi, qi, *_: (gi, 0, 0, 0)),
                pl.BlockSpec((None, c, LANES), lambda bi, gi, qi, *_: (bi, qi, gi)),
                ospec, ospec]
    return _attn_call(
        "win", [far], [p3, p3, p3, tiles, gates3, o_cmp, o_slc], in_specs, ospec,
        jax.ShapeDtypeStruct((b, t, NSA_HEADS * dh), BF16), (b, g, t // c),
        c=c, ns=hpg, gr=c, dk=dh, dv=dh)


def diff_attn(p3, tiles, far, lam, g2, *, c=ATT_C):
    b, t, _ = p3.shape
    dh = HEAD_DIM
    qb, kb, vb = 20, 28, 36
    in_specs = [pl.BlockSpec((None, c, dh), lambda bi, hi, qi, *_: (bi, qi, qb + hi)),
                pl.BlockSpec((None, t, dh), lambda bi, hi, qi, *_: (bi, 0, kb + hi)),
                pl.BlockSpec((None, t, dh), lambda bi, hi, qi, *_: (bi, 0, vb + hi)),
                pl.BlockSpec((1, 2, c, c), lambda bi, hi, qi, *_: (hi, 0, 0, 0)),
                pl.BlockSpec((1, dh), lambda bi, hi, qi, *_: (0, 0))]
    return _attn_call(
        "diff", [far, lam], [p3, p3, p3, tiles, g2], in_specs,
        pl.BlockSpec((None, c, dh), lambda bi, hi, qi, *_: (bi, qi, hi)),
        jax.ShapeDtypeStruct((b, t, DIFF_HEADS * dh), BF16), (b, DIFF_HEADS, t // c),
        c=c, ns=2, gr=c, dk=dh, dv=dh)


def mla_attn(qf, kf, kv3, *, c=MLA_C):
    b, t, _ = qf.shape
    dk = 2 * HEAD_DIM
    c = min(c, t)
    in_specs = [pl.BlockSpec((None, c, dk), lambda bi, hi, qi: (bi, qi, hi)),
                pl.BlockSpec((None, t, dk), lambda bi, hi, qi: (bi, 0, hi)),
                pl.BlockSpec((None, t, MLA_V), lambda bi, hi, qi: (bi, 0, MLA_HEADS + hi))]
    return _attn_call(
        "mla", [], [qf, kf, kv3], in_specs,
        pl.BlockSpec((None, c, MLA_V), lambda bi, hi, qi: (bi, qi, hi)),
        jax.ShapeDtypeStruct((b, t, MLA_HEADS * MLA_V), BF16), (b, MLA_HEADS, t // c),
        c=c, ns=2, gr=c // 2, dk=dk, dv=MLA_V)


def _mla_assemble_kernel(q_ref, kv_ref, c_ref, cos_ref, sin_ref, qf_ref, kf_ref):
    h16 = MLA_HEADS
    cos = cos_ref[...]
    sin = sin_ref[...]
    lane = lax.broadcasted_iota(I32, cos.shape, 1)
    first = lane < MLA_ROPE
    kpe = c_ref[:, 0:LANES] * cos + c_ref[:, LANES:2 * LANES] * sin
    for j in range(h16 // 2):
        pe = q_ref[:, (h16 + j) * LANES:(h16 + j + 1) * LANES]
        pep = q_ref[:, (h16 + h16 // 2 + j) * LANES:(h16 + h16 // 2 + j + 1) * LANES]
        qpe = pe * cos + pep * sin
        for par in range(2):
            h = 2 * j + par
            keep = first if par == 0 else jnp.logical_not(first)
            qf_ref[:, (2 * h) * LANES:(2 * h + 1) * LANES] = q_ref[:, h * LANES:(h + 1) * LANES].astype(BF16)
            qf_ref[:, (2 * h + 1) * LANES:(2 * h + 2) * LANES] = jnp.where(keep, qpe, 0.0).astype(BF16)
            kf_ref[:, (2 * h) * LANES:(2 * h + 1) * LANES] = kv_ref[:, h * LANES:(h + 1) * LANES].astype(BF16)
            kf_ref[:, (2 * h + 1) * LANES:(2 * h + 2) * LANES] = jnp.where(keep, kpe, 0.0).astype(BF16)


def mla_assemble(qraw, kvraw, craw, cos2, sin2, seq, *, tm=256):
    n = qraw.shape[0]
    tm = min(tm, seq)
    nt = seq // tm
    wq = qraw.shape[1]
    return pl.pallas_call(
        _mla_assemble_kernel,
        grid=(n // tm,),
        in_specs=[pl.BlockSpec((tm, wq), lambda i: (i, 0)),
                  pl.BlockSpec((tm, MLA_HEADS * LANES), lambda i: (i, 0)),
                  pl.BlockSpec((tm, 2 * LANES), lambda i: (i, 4)),
                  pl.BlockSpec((tm, LANES), lambda i: (i % nt, 0)),
                  pl.BlockSpec((tm, LANES), lambda i: (i % nt, 0))],
        out_specs=[pl.BlockSpec((tm, 2 * MLA_HEADS * LANES), lambda i: (i, 0)),
                   pl.BlockSpec((tm, 2 * MLA_HEADS * LANES), lambda i: (i, 0))],
        out_shape=[jax.ShapeDtypeStruct((n, 2 * MLA_HEADS * LANES), BF16)] * 2,
        compiler_params=_cparams(1),
        name="mla_assemble",
    )(qraw, kvraw, craw, cos2, sin2)


def _mem_attn_kernel(q_ref, kv_ref, o_ref):
    hd = MEM_HEAD_DIM
    voff = MEM_HEADS * hd
    for h in range(MEM_HEADS):
        sl = slice(h * hd, (h + 1) * hd)
        s = _dot_nt(q_ref[:, sl], kv_ref[:, sl])
        m = jnp.max(s, axis=-1, keepdims=True)
        p = jnp.exp(s - m)
        p = p / jnp.sum(p, axis=-1, keepdims=True)
        o_ref[:, sl] = _dot(p.astype(BF16), kv_ref[:, voff + h * hd:voff + (h + 1) * hd]).astype(o_ref.dtype)


def mem_attn(q3, kv3, *, tq=512):
    b, t, w = q3.shape
    m = kv3.shape[1]
    tq = min(tq, t)
    return pl.pallas_call(
        _mem_attn_kernel,
        grid=(b, t // tq),
        in_specs=[pl.BlockSpec((None, tq, w), lambda bi, qi: (bi, qi, 0)),
                  pl.BlockSpec((None, m, 2 * w), lambda bi, qi: (bi, 0, 0))],
        out_specs=pl.BlockSpec((None, tq, w), lambda bi, qi: (bi, qi, 0)),
        out_shape=jax.ShapeDtypeStruct((b, t, w), BF16),
        compiler_params=_cparams(2),
        name="mem_attn",
    )(q3, kv3)


def _router_kernel(h_ref, g_ref, rw_ref, rb_ref, hn_ref, ro_ref):
    x = h_ref[...]
    hn = x * lax.rsqrt(jnp.mean(x * x, axis=-1, keepdims=True) + RMS_EPS) * g_ref[...]
    hn_ref[...] = hn
    logits = jnp.dot(hn, rw_ref[...], precision=HIGHEST, preferred_element_type=F32) + rb_ref[...]
    lane = lax.broadcasted_iota(I32, logits.shape, 1)
    lg = jnp.where(lane < N_EXPERTS, logits, -jnp.inf)
    v1 = jnp.max(lg, axis=-1, keepdims=True)
    i1 = jnp.min(jnp.where(lg == v1, lane, LANES), axis=-1, keepdims=True)
    lg2 = jnp.where(lane == i1, -jnp.inf, lg)
    v2 = jnp.max(lg2, axis=-1, keepdims=True)
    i2 = jnp.min(jnp.where(lg2 == v2, lane, LANES), axis=-1, keepdims=True)
    e2 = jnp.exp(v2 - v1)
    w1 = 1.0 / (1.0 + e2)
    w2 = e2 / (1.0 + e2)
    ro_ref[...] = jnp.where(lane == 0, i1.astype(F32),
                            jnp.where(lane == 1, i2.astype(F32),
                                      jnp.where(lane == 2, w1, jnp.where(lane == 3, w2, 0.0))))


def moe_router(h, g, rw, rb, *, tm=256):
    n, d = h.shape
    rwp = jnp.zeros((d, LANES), F32).at[:, :N_EXPERTS].set(rw.astype(F32))
    rbp = jnp.zeros((1, LANES), F32).at[0, :N_EXPERTS].set(rb.astype(F32))
    return pl.pallas_call(
        _router_kernel,
        grid=(n // tm,),
        in_specs=[pl.BlockSpec((tm, d), lambda i: (i, 0)),
                  pl.BlockSpec((1, d), lambda i: (0, 0)),
                  pl.BlockSpec((d, LANES), lambda i: (0, 0)),
                  pl.BlockSpec((1, LANES), lambda i: (0, 0))],
        out_specs=[pl.BlockSpec((tm, d), lambda i: (i, 0)),
                   pl.BlockSpec((tm, LANES), lambda i: (i, 0))],
        out_shape=[jax.ShapeDtypeStruct((n, d), F32), jax.ShapeDtypeStruct((n, LANES), F32)],
        compiler_params=_cparams(1),
        name="moe_router",
    )(h, g.reshape(1, d).astype(F32), rwp, rbp)


def _row_copy(src, dst, sem):
    return pltpu.make_async_copy(src, dst, sem)


def _dispatch_kernel(d1_ref, d2_ref, x_ref, xs_in_ref, xs_ref, sem):
    del xs_in_ref
    tm = x_ref.shape[0]
    base = pl.program_id(0) * tm

    def issue(r, carry):
        src = x_ref.at[pl.ds(r, 1), :]
        _row_copy(src, xs_ref.at[pl.ds(d1_ref[base + r], 1), :], sem).start()
        _row_copy(src, xs_ref.at[pl.ds(d2_ref[base + r], 1), :], sem).start()
        return carry

    def drain(r, carry):
        src = x_ref.at[pl.ds(r, 1), :]
        _row_copy(src, xs_ref.at[pl.ds(0, 1), :], sem).wait()
        _row_copy(src, xs_ref.at[pl.ds(0, 1), :], sem).wait()
        return carry

    lax.fori_loop(0, tm, issue, 0)
    lax.fori_loop(0, tm, drain, 0)


def moe_dispatch(hn, d1, d2, rows, *, tm=256):
    n, d = hn.shape
    return pl.pallas_call(
        _dispatch_kernel,
        grid_spec=pltpu.PrefetchScalarGridSpec(
            num_scalar_prefetch=2, grid=(n // tm,),
            in_specs=[pl.BlockSpec((tm, d), lambda i, *_: (i, 0)),
                      pl.BlockSpec(memory_space=pl.ANY)],
            out_specs=pl.BlockSpec(memory_space=pl.ANY),
            scratch_shapes=[pltpu.SemaphoreType.DMA(())]),
        out_shape=jax.ShapeDtypeStruct((rows, d), hn.dtype),
        input_output_aliases={3: 0},
        compiler_params=_cparams(1),
        name="moe_dispatch",
    )(d1, d2, hn, jnp.zeros((rows, d), hn.dtype))


def _combine_kernel(d1_ref, d2_ref, ys_ref, h_ref, ro_ref, o_ref, b1_ref, b2_ref, sem):
    tm = h_ref.shape[0]
    base = pl.program_id(0) * tm

    def issue(r, carry):
        _row_copy(ys_ref.at[pl.ds(d1_ref[base + r], 1), :], b1_ref.at[pl.ds(r, 1), :], sem).start()
        _row_copy(ys_ref.at[pl.ds(d2_ref[base + r], 1), :], b2_ref.at[pl.ds(r, 1), :], sem).start()
        return carry

    def drain(r, carry):
        _row_copy(ys_ref.at[pl.ds(0, 1), :], b1_ref.at[pl.ds(r, 1), :], sem).wait()
        _row_copy(ys_ref.at[pl.ds(0, 1), :], b2_ref.at[pl.ds(r, 1), :], sem).wait()
        return carry

    lax.fori_loop(0, tm, issue, 0)
    lax.fori_loop(0, tm, drain, 0)
    ro = ro_ref[...]
    o_ref[...] = h_ref[...] + ro[:, 2:3] * b1_ref[...] + ro[:, 3:4] * b2_ref[...]


def moe_combine(ys, h, ro, d1, d2, *, tm=256):
    n, d = h.shape
    return pl.pallas_call(
        _combine_kernel,
        grid_spec=pltpu.PrefetchScalarGridSpec(
            num_scalar_prefetch=2, grid=(n // tm,),
            in_specs=[pl.BlockSpec(memory_space=pl.ANY),
                      pl.BlockSpec((tm, d), lambda i, *_: (i, 0)),
                      pl.BlockSpec((tm, LANES), lambda i, *_: (i, 0))],
            out_specs=pl.BlockSpec((tm, d), lambda i, *_: (i, 0)),
            scratch_shapes=[pltpu.VMEM((tm, d), F32), pltpu.VMEM((tm, d), F32),
                            pltpu.SemaphoreType.DMA(())]),
        out_shape=jax.ShapeDtypeStruct((n, d), F32),
        compiler_params=_cparams(1),
        name="moe_combine",
    )(d1, d2, ys, h, ro)


def moe_block(h, g, rw, rb, w_gu, w_down, layer, *, tm_e=EXPERT_TM):
    n, d = h.shape
    tm_e = min(tm_e, n)
    hn, ro = moe_router(h, g, rw, rb)
    e1 = ro[:, 0].astype(I32)
    e2 = ro[:, 1].astype(I32)
    ind = (jax.nn.one_hot(e1, N_EXPERTS, dtype=I32) + jax.nn.one_hot(e2, N_EXPERTS, dtype=I32))
    csum = jnp.cumsum(ind, axis=0)
    rank = csum - ind
    cnt = csum[-1]
    pcnt = (cnt + tm_e - 1) // tm_e * tm_e
    ends = jnp.cumsum(pcnt)
    off = ends - pcnt
    d1 = (off[e1] + jnp.take_along_axis(rank, e1[:, None], axis=1)[:, 0]).astype(I32)
    d2 = (off[e2] + jnp.take_along_axis(rank, e2[:, None], axis=1)[:, 0]).astype(I32)
    rows = 2 * n + N_EXPERTS * tm_e
    tile_start = jnp.arange(rows // tm_e, dtype=I32) * tm_e
    tile_e = jnp.minimum(jnp.sum((tile_start[:, None] >= ends[None, :]).astype(I32), axis=1),
                         N_EXPERTS - 1).astype(I32)
    xs = moe_dispatch(hn, d1, d2, rows)
    tile_w = tile_e + layer * N_EXPERTS
    act = gmm(xs, w_gu, tile_w, tm=tm_e, tn=512, out_dtype=BF16, swiglu=True)
    ys = gmm(act, w_down, tile_w, tm=tm_e, tn=512, out_dtype=F32)
    return moe_combine(ys, h, ro, d1, d2)


def _mem_block(h, mem2, b, t, i, g_ca, g_mem, wq, wkv, wo):
    n, d = h.shape
    hn = rmsnorm(h, g_ca[i], BF16)
    memn = rmsnorm(mem2, g_mem[i], BF16)
    q = mm(hn, (wq[i] * MEM_HEAD_DIM ** -0.5).astype(BF16), out_dtype=BF16)
    kv = mm(memn, wkv, group=i, out_dtype=BF16)
    o = mem_attn(q.reshape(b, t, -1), kv.reshape(b, mem2.shape[0] // b, -1))
    return mm(o.reshape(n, -1), wo, group=i, out_dtype=F32, res=h)


def _even_layer(h, b, t, li, lam_init, p, tabs):
    n, d = h.shape
    w_in = p["even_w_in"][li]
    nq, nkv = NSA_HEADS * HEAD_DIM, NSA_KV_HEADS * HEAD_DIM
    g0 = nq + 6 * nkv
    dq0 = g0 + 3 * NSA_HEADS
    dqw = DIFF_HEADS * 2 * DIFF_QK_DIM
    w_main = jnp.concatenate([w_in[:, :nq] * HEAD_DIM ** -0.5, w_in[:, nq:g0],
                              w_in[:, dq0:dq0 + dqw] * DIFF_QK_DIM ** -0.5, w_in[:, dq0 + dqw:]],
                             axis=1).astype(BF16)
    w_gate = jnp.zeros((d, NSA_KV_HEADS * LANES), F32)
    for g in range(NSA_KV_HEADS):
        w_gate = w_gate.at[:, g * LANES:g * LANES + 3 * NSA_HPG].set(
            w_in[:, g0 + g * 3 * NSA_HPG:g0 + (g + 1) * 3 * NSA_HPG])
    xn = rmsnorm(h, p["norm_mix"][2 * li], BF16)
    proj = mm(xn, w_main, out_dtype=BF16)
    gates = mm(xn, w_gate.astype(BF16), out_dtype=F32, tn=256)
    p3 = proj.reshape(b, t, -1)

    kvs = p3[:, :, nq:nq + 2 * nkv].reshape(b, t, 2, NSA_KV_HEADS, HEAD_DIM)
    acat = kvs.transpose(2, 0, 3, 1, 4).reshape(2, b, NSA_KV_HEADS, t // CMP_STRIDE, CMP_STRIDE * HEAD_DIM)
    w1 = p["nsa_cmp_w1"][li]
    half = CMP_STRIDE * HEAD_DIM
    w1cat = jnp.concatenate([w1[:, :half], w1[:, half:]], axis=2).astype(BF16)
    pos = p["nsa_cmp_pos"][li].reshape(2, 2, half)
    pos2 = jnp.zeros((2, 8, half), F32).at[:, :2].set(pos).astype(BF16)
    kvc = nsa_compress(acat, w1cat, p["nsa_cmp_w2"][li].astype(BF16), pos2)
    o_cmp, sel = nsa_cmp_select(p3, kvc, tabs["cbias"], tabs["overlap"])
    o_slc = nsa_slc(p3, sel, tabs["expand"], tabs["tiles_a"], tabs["far_a"])
    o_a = nsa_win(p3, gates.reshape(b, t, -1), o_cmp, o_slc, tabs["tiles_a"], tabs["far_a"])

    lq1, lk1, lq2, lk2 = p["diff_lambda"][li].astype(F32)
    lam = jnp.exp(jnp.sum(lq1 * lk1)) - jnp.exp(jnp.sum(lq2 * lk2)) + lam_init
    g2 = (p["diff_subln"][li].astype(F32) * (1.0 - lam_init)).reshape(1, HEAD_DIM)
    o_b = diff_attn(p3, tabs["tiles_b"], tabs["far_b"], lam.reshape(1), g2)

    mix_in = jnp.concatenate([o_a, o_b], axis=-1).reshape(n, -1)
    h = mm(mix_in, p["even_w_out"], group=li, out_dtype=F32, res=h)
    return h


def _odd_layer(h, b, t, li, p, tabs):
    n, d = h.shape
    hh, nope, rp = MLA_HEADS, MLA_NOPE, MLA_ROPE
    scale = (nope + rp) ** -0.5

    def rot(w):
        w = w.reshape(w.shape[0], -1, 2, rp // 2)
        return jnp.stack([-w[:, :, 1], w[:, :, 0]], axis=2).reshape(w.shape[0], -1)

    w_in = p["odd_w_in"][li]
    kpe_w = w_in[:, MLA_Q_RANK + MLA_KV_RANK:]
    w_in_ext = jnp.concatenate([w_in[:, :MLA_Q_RANK + MLA_KV_RANK], kpe_w, kpe_w, rot(kpe_w), rot(kpe_w)],
                               axis=1).astype(BF16)
    wq = (p["mla_w_q_up"][li] * scale).reshape(MLA_Q_RANK, hh, nope + rp)
    wq_pe = wq[:, :, nope:].reshape(MLA_Q_RANK, hh * rp)
    wq_ext = jnp.concatenate([wq[:, :, :nope].reshape(MLA_Q_RANK, hh * nope), wq_pe, rot(wq_pe)],
                             axis=1).astype(BF16)
    wkv = p["mla_w_kv_up"][li].reshape(MLA_KV_RANK, hh, nope + MLA_V)
    wkv_ext = jnp.concatenate([wkv[:, :, :nope].reshape(MLA_KV_RANK, -1),
                               wkv[:, :, nope:].reshape(MLA_KV_RANK, -1)], axis=1).astype(BF16)

    xn = rmsnorm(h, p["norm_mix"][2 * li + 1], BF16)
    craw = mm(xn, w_in_ext, out_dtype=F32, tn=256)
    cqn = rmsnorm(craw, p["mla_q_norm"][li], BF16, width=MLA_Q_RANK, col_block=0)
    ckvn = rmsnorm(craw, p["mla_kv_norm"][li], BF16, width=MLA_KV_RANK, col_block=1)
    qraw = mm(cqn, wq_ext, out_dtype=F32)
    kvraw = mm(ckvn, wkv_ext, out_dtype=BF16)
    qf, kf = mla_assemble(qraw, kvraw, craw, tabs["cos2"], tabs["sin2"], t)
    o = mla_attn(qf.reshape(b, t, -1), kf.reshape(b, t, -1), kvraw.reshape(b, t, -1))
    return mm(o.reshape(n, -1), p["odd_w_out"], group=li, out_dtype=F32, res=h)


def _tables(rel_bias, t):
    bias_a, bias_b = rel_bias[:, :NSA_HEADS], rel_bias[:, NSA_HEADS:]
    tiles_a, far_a = _bias_tiles(bias_a, ATT_C)
    tiles_b, far_b = _bias_tiles(bias_b, ATT_C)
    ncb = t // CMP_STRIDE
    nslc = t // SLC_BLOCK
    tt = jnp.arange(t)
    blk_start = jnp.arange(ncb) * CMP_STRIDE
    cbias = _bias_lookup(tt[:, None] - (blk_start + CMP_BLOCK - 1)[None, :], bias_a)
    jj = jnp.arange(nslc)
    n_cmp = (t - CMP_BLOCK) // CMP_STRIDE + 1
    overlap = ((blk_start[:, None] < (jj[None, :] + 1) * SLC_BLOCK)
               & (blk_start[:, None] + CMP_BLOCK > jj[None, :] * SLC_BLOCK)
               & (jnp.arange(ncb)[:, None] < n_cmp)).astype(F32)
    expand = (jj[:, None] == (tt[None, :] // SLC_BLOCK)).astype(BF16)
    inv_freq = ROPE_THETA ** (-jnp.arange(0, MLA_ROPE, 2, dtype=F32) / MLA_ROPE)
    ang = jnp.arange(t, dtype=F32)[:, None] * inv_freq
    cos2 = jnp.tile(jnp.cos(ang), (1, 2 * LANES // MLA_ROPE))
    sin2 = jnp.tile(jnp.sin(ang), (1, 2 * LANES // MLA_ROPE))
    return dict(tiles_a=tiles_a, far_a=far_a, tiles_b=tiles_b, far_b=far_b, cbias=cbias,
                overlap=overlap, expand=expand, cos2=cos2, sin2=sin2)


def kernel(x, mem, rel_bias, norm_mix, norm_mem, norm_ca, norm_ffn, norm_final, even_w_in, even_w_out, nsa_cmp_pos, nsa_cmp_w1, nsa_cmp_w2, diff_lambda, diff_subln, ffn_w_gu, ffn_w_down, odd_w_in, mla_q_norm, mla_kv_norm, mla_w_q_up, mla_w_kv_up, odd_w_out, router_w, router_b, moe_w_gu, moe_w_down, ca_wq, ca_wkv, ca_wo):
    p = dict(norm_mix=norm_mix, even_w_in=even_w_in, even_w_out=even_w_out, nsa_cmp_pos=nsa_cmp_pos,
             nsa_cmp_w1=nsa_cmp_w1, nsa_cmp_w2=nsa_cmp_w2, diff_lambda=diff_lambda, diff_subln=diff_subln,
             odd_w_in=odd_w_in, mla_q_norm=mla_q_norm, mla_kv_norm=mla_kv_norm, mla_w_q_up=mla_w_q_up,
             mla_w_kv_up=mla_w_kv_up, odd_w_out=odd_w_out)
    b, t, d = x.shape
    depth = norm_mix.shape[0]
    tabs = _tables(rel_bias, t)
    h = x.reshape(b * t, d).astype(F32)
    mem2 = mem.reshape(-1, d).astype(F32)
    for i in range(depth):
        li = i // 2
        if i % 2 == 0:
            h = _even_layer(h, b, t, li, 0.8 - 0.6 * math.exp(-0.3 * i), p, tabs)
        else:
            h = _odd_layer(h, b, t, li, p, tabs)
        h = _mem_block(h, mem2, b, t, i, norm_ca, norm_mem, ca_wq, ca_wkv, ca_wo)
        if i % 2 == 0:
            hn = rmsnorm(h, norm_ffn[i], BF16)
            act = mm(hn, ffn_w_gu, group=li, out_dtype=BF16, swiglu=True)
            h = mm(act, ffn_w_down, group=li, out_dtype=F32, res=h)
        else:
            h = moe_block(h, norm_ffn[i], router_w[li], router_b[li],
                          moe_w_gu.reshape((-1,) + moe_w_gu.shape[2:]),
                          moe_w_down.reshape((-1,) + moe_w_down.shape[2:]), li)
    return rmsnorm(h, norm_final, x.dtype).reshape(b, t, d)
```

```python
import functools
import math

import jax
import jax.numpy as jnp
from jax import lax
from jax.experimental import pallas as pl
from jax.experimental.pallas import tpu as pltpu

F32 = jnp.float32
BF16 = jnp.bfloat16
I32 = jnp.int32

HEAD_DIM = 128
NSA_HEADS = 8
NSA_KV_HEADS = 2
NSA_HPG = NSA_HEADS // NSA_KV_HEADS
CMP_BLOCK = 32
CMP_STRIDE = 16
SLC_BLOCK = 64
SLC_TOPK = 16
WINDOW = 512
DIFF_HEADS = 8
DIFF_QK_DIM = 64
MLA_HEADS = 16
MLA_Q_RANK = 512
MLA_KV_RANK = 512
MLA_NOPE = 128
MLA_ROPE = 64
MLA_V = 128
ROPE_THETA = 10000.0
MEM_HEADS = 4
MEM_HEAD_DIM = 128
N_EXPERTS = 8
NUM_BUCKETS = 32
MAX_DISTANCE = 128
RMS_EPS = 1e-6
NEG_INF = -1e30
BIG = 1e30

LANES = 128
MXU_COLS = 256
GMM_VMEM_BUDGET = 46 * 1024 * 1024
VMEM_LIMIT = 56 * 1024 * 1024
ATT_C = 256
MLA_C = 512
EXPERT_TM = 512
HIGHEST = lax.Precision.HIGHEST


def _cparams(n_axes):
    return pltpu.CompilerParams(dimension_semantics=("arbitrary",) * n_axes,
                                vmem_limit_bytes=VMEM_LIMIT)


def _dot(a, b):
    return jnp.dot(a, b, preferred_element_type=F32)


def _rep(x, n):
    return x if n == 1 else jnp.concatenate([x] * n, axis=1)


def _rep_cat(xs):
    return xs[0] if len(xs) == 1 else jnp.concatenate(xs, axis=1)


def _dot_nt(a, b):
    return lax.dot_general(a, b, (((1,), (1,)), ((), ())), preferred_element_type=F32)


def _rms_kernel(x_ref, g_ref, o_ref):
    x = x_ref[...].astype(F32)
    y = x * lax.rsqrt(jnp.mean(x * x, axis=-1, keepdims=True) + RMS_EPS)
    o_ref[...] = (y * g_ref[...]).astype(o_ref.dtype)


def rmsnorm(x, g, out_dtype, width=None, col_block=0, tm=512):
    m = x.shape[0]
    width = width or x.shape[1]
    tm = min(tm, m)
    return pl.pallas_call(
        _rms_kernel,
        grid=(m // tm,),
        in_specs=[pl.BlockSpec((tm, width), lambda i: (i, col_block)),
                  pl.BlockSpec((1, width), lambda i: (0, 0))],
        out_specs=pl.BlockSpec((tm, width), lambda i: (i, 0)),
        out_shape=jax.ShapeDtypeStruct((m, width), out_dtype),
        compiler_params=_cparams(1),
        name="rmsnorm",
    )(x, g.reshape(1, width).astype(F32))


def _gmm_kernel(te_ref, nv_ref, x_ref, w_ref, *rest, swiglu, has_res, cast_w):
    nw = 2 if swiglu else 1
    w_refs = (w_ref,) + tuple(rest[:nw - 1])
    res_ref = rest[nw - 1] if has_res else None
    o_ref = rest[nw - 1 + int(has_res)]
    i = pl.program_id(1)
    if cast_w:
        wb_refs = rest[nw + int(has_res):]
        prev = te_ref[jnp.maximum(i - 1, 0)]

        @pl.when((i == 0) | (te_ref[i] != prev))
        def _():
            for src, dst in zip(w_refs, wb_refs):
                dst[...] = src[...].astype(BF16)
        w_refs = wb_refs

    @pl.when(i < nv_ref[0])
    def _():
        x = x_ref[...].astype(BF16)
        acc = _dot(x, w_refs[0][...])
        if swiglu:
            acc = acc * jax.nn.sigmoid(acc) * _dot(x, w_refs[1][...])
        if has_res:
            acc = acc + res_ref[...]
        o_ref[...] = acc.astype(o_ref.dtype)

    @pl.when(i >= nv_ref[0])
    def _():
        o_ref[...] = jnp.zeros(o_ref.shape, o_ref.dtype)


def _gmm_tn(k, n, tm, x_bytes, w_bytes, out_bytes, n_w, has_res):
    best = None
    for tn in range(LANES, n + 1, LANES):
        if n % tn or (tn % MXU_COLS and tn != n):
            continue
        fp = (n_w * k * tn * w_bytes * 2 + (n_w * k * tn * 2 if w_bytes == 4 else 0)
              + tm * k * x_bytes * 2 + tm * tn * out_bytes * 2 + tm * tn * 4 * n_w
              + (tm * tn * 4 * 2 if has_res else 0))
        if fp <= GMM_VMEM_BUDGET:
            best = tn
    assert best is not None, (k, n, tm)
    return best


def gmm(x, w, tile_e, n_valid, *, tm, out_dtype, swiglu=False, res=None):
    r, k = x.shape
    n = w.shape[2] // 2 if swiglu else w.shape[2]
    n_w = 2 if swiglu else 1
    cast_w = w.dtype != BF16
    tn = _gmm_tn(k, n, tm, x.dtype.itemsize, w.dtype.itemsize, jnp.dtype(out_dtype).itemsize, n_w,
                 res is not None)
    nj = n // tn
    in_specs = [pl.BlockSpec((tm, k), lambda j, i, te, nv: (i, 0)),
                pl.BlockSpec((None, k, tn), lambda j, i, te, nv: (te[i], 0, j))]
    args = [x, w]
    if swiglu:
        in_specs.append(pl.BlockSpec((None, k, tn), lambda j, i, te, nv: (te[i], 0, j + nj)))
        args.append(w)
    if res is not None:
        in_specs.append(pl.BlockSpec((tm, tn), lambda j, i, te, nv: (i, j)))
        args.append(res)
    scratch = [pltpu.VMEM((k, tn), BF16)] * (n_w if cast_w else 0)
    return pl.pallas_call(
        functools.partial(_gmm_kernel, swiglu=swiglu, has_res=res is not None, cast_w=cast_w),
        grid_spec=pltpu.PrefetchScalarGridSpec(
            num_scalar_prefetch=2,
            grid=(nj, r // tm),
            in_specs=in_specs,
            out_specs=pl.BlockSpec((tm, tn), lambda j, i, te, nv: (i, j)),
            scratch_shapes=scratch),
        out_shape=jax.ShapeDtypeStruct((r, n), out_dtype),
        compiler_params=_cparams(2),
        name="gmm_swiglu" if swiglu else "gmm",
    )(tile_e, n_valid, *args)


def mm(x, w, *, out_dtype, tm=512, swiglu=False, res=None, group=0):
    m = x.shape[0]
    tm = min(tm, m)
    if w.ndim == 2:
        w = w[None]
    return gmm(x, w, jnp.full((m // tm,), group, I32), jnp.full((1,), m // tm, I32), tm=tm,
               out_dtype=out_dtype, swiglu=swiglu, res=res)


def _t5_bucket(dist):
    n = jnp.maximum(dist, 0)
    max_exact = NUM_BUCKETS // 2
    nf = jnp.maximum(n, 1).astype(F32)
    large = max_exact + (jnp.log(nf / max_exact) / math.log(MAX_DISTANCE / max_exact)
                         * (NUM_BUCKETS - max_exact)).astype(I32)
    large = jnp.minimum(large, NUM_BUCKETS - 1)
    return jnp.where(n < max_exact, n, large)


def _bias_lookup(dist, tbl):
    oh = jax.nn.one_hot(_t5_bucket(dist), NUM_BUCKETS, dtype=F32)
    return jnp.einsum("...k,kh->h...", oh, tbl.astype(F32), precision=HIGHEST)


def _bias_tiles(tbl, c):
    i = jnp.arange(c)[:, None]
    j = jnp.arange(c)[None, :]
    tiles = jnp.stack([_bias_lookup(i - j, tbl), _bias_lookup(c + i - j, tbl)], axis=1)
    far = _bias_lookup(jnp.asarray(2 * c), tbl)
    assert c + 1 >= MAX_DISTANCE
    return tiles, far


def _compress_kernel(a_ref, w1_ref, w2_ref, pos_ref, o_ref):
    nblk = a_ref.shape[0]
    pq = _dot(a_ref[...], w1_ref[...])
    r = _dot(pos_ref[...], w1_ref[...])
    c = r[0:1, :HEAD_DIM] + r[1:2, HEAD_DIM:]
    hp = pq[:, :HEAD_DIM] + pltpu.roll(pq[:, HEAD_DIM:], nblk - 1, 0) + c
    hdn = jax.nn.gelu(hp)
    o_ref[...] = _dot(hdn.astype(BF16), w2_ref[...]).astype(o_ref.dtype)


def nsa_compress(acat, w1cat, w2, pos2):
    _, b, g, nblk, wid = acat.shape
    return pl.pallas_call(
        _compress_kernel,
        grid=(2, b, g),
        in_specs=[pl.BlockSpec((None, None, None, nblk, wid), lambda s, bi, gi: (s, bi, gi, 0, 0)),
                  pl.BlockSpec((None, wid, 2 * HEAD_DIM), lambda s, bi, gi: (s, 0, 0)),
                  pl.BlockSpec((None, HEAD_DIM, HEAD_DIM), lambda s, bi, gi: (s, 0, 0)),
                  pl.BlockSpec((None, 8, wid), lambda s, bi, gi: (s, 0, 0))],
        out_specs=pl.BlockSpec((None, None, None, nblk, HEAD_DIM), lambda s, bi, gi: (s, bi, gi, 0, 0)),
        out_shape=jax.ShapeDtypeStruct((2, b, g, nblk, HEAD_DIM), BF16),
        compiler_params=_cparams(3),
        name="nsa_compress",
    )(acat, w1cat, w2, pos2)


def _nsa_cmp_kernel(q_ref, kc_ref, vc_ref, cb_ref, ov_ref, o_ref, sel_ref, *, tq, topk):
    qi = pl.program_id(2)
    ncb = kc_ref.shape[0]
    nslc = sel_ref.shape[1]
    t = qi * tq + lax.broadcasted_iota(I32, (tq, ncb), 0)
    n = lax.broadcasted_iota(I32, (tq, ncb), 1)
    mask = n * CMP_STRIDE + (CMP_BLOCK - 1) <= t
    kc = kc_ref[...]
    vc = vc_ref[...]
    psum = jnp.zeros((tq, ncb), F32)
    for h in range(NSA_HPG):
        qh = q_ref[:, h * HEAD_DIM:(h + 1) * HEAD_DIM]
        s = jnp.where(mask, _dot_nt(qh, kc) + cb_ref[h], NEG_INF)
        m = jnp.max(s, axis=-1, keepdims=True)
        p = jnp.where(mask, jnp.exp(s - m), 0.0)
        p = p / jnp.maximum(jnp.sum(p, axis=-1, keepdims=True), 1e-30)
        o_ref[:, h * HEAD_DIM:(h + 1) * HEAD_DIM] = _dot(p.astype(BF16), vc).astype(o_ref.dtype)
        psum = psum + p
    imp = jnp.dot(psum, ov_ref[...], precision=HIGHEST, preferred_element_type=F32)
    tt = qi * tq + lax.broadcasted_iota(I32, (tq, nslc), 0)
    j = lax.broadcasted_iota(I32, (tq, nslc), 1)
    cur = tt // SLC_BLOCK
    forced = (j == 0) | (j == cur) | (j == cur - 1)
    visible = j * SLC_BLOCK <= tt
    score = jnp.where(forced, BIG, jnp.where(visible, imp, NEG_INF))
    rank = jnp.zeros((tq, nslc), F32)
    for i in range(nslc):
        col = score[:, i:i + 1]
        tie = jnp.where(j > i, 1.0, 0.0)
        rank = rank + jnp.where(col > score, 1.0, jnp.where(col == score, tie, 0.0))
    sel_ref[...] = jnp.where(rank < topk, 1.0, 0.0).astype(sel_ref.dtype)


def nsa_cmp_select(p3, kvc, cbias, overlap, *, tq=ATT_C):
    b, t, _ = p3.shape
    g = NSA_KV_HEADS
    ncb = kvc.shape[3]
    nslc = t // SLC_BLOCK
    gw = NSA_HPG * HEAD_DIM
    return pl.pallas_call(
        functools.partial(_nsa_cmp_kernel, tq=tq, topk=min(SLC_TOPK, nslc)),
        grid=(b, g, t // tq),
        in_specs=[pl.BlockSpec((None, tq, gw), lambda bi, gi, qi: (bi, qi, gi)),
                  pl.BlockSpec((None, None, None, ncb, HEAD_DIM), lambda bi, gi, qi: (0, bi, gi, 0, 0)),
                  pl.BlockSpec((None, None, None, ncb, HEAD_DIM), lambda bi, gi, qi: (1, bi, gi, 0, 0)),
                  pl.BlockSpec((NSA_HPG, tq, ncb), lambda bi, gi, qi: (gi, qi, 0)),
                  pl.BlockSpec((ncb, nslc), lambda bi, gi, qi: (0, 0))],
        out_specs=[pl.BlockSpec((None, tq, gw), lambda bi, gi, qi: (bi, qi, gi)),
                   pl.BlockSpec((None, None, tq, nslc), lambda bi, gi, qi: (bi, gi, qi, 0))],
        out_shape=[jax.ShapeDtypeStruct((b, t, NSA_HEADS * HEAD_DIM), F32),
                   jax.ShapeDtypeStruct((b, g, t, nslc), BF16)],
        compiler_params=_cparams(3),
        name="nsa_cmp_select",
    )(p3, kvc, kvc, cbias, overlap)


def _attn_kernel(*refs, cfg):
    kind = cfg["kind"]
    c = cfg["c"]
    ns = cfg["ns"]
    has_bias = kind != "mla"
    it = iter(refs)
    far_ref = next(it) if has_bias else None
    lam_ref = next(it) if kind == "diff" else None
    q_ref, k_ref, v_ref = next(it), next(it), next(it)
    bt_ref = next(it) if has_bias else None
    if kind == "slc":
        sel_ref, e_ref = next(it), next(it)
    if kind == "win":
        gl_ref, ocmp_ref, oslc_ref = next(it), next(it), next(it)
    if kind == "diff":
        g2_ref = next(it)
    o_ref = next(it)
    qs_ref = next(it) if kind != "mla" else None
    m_ref, acc_ref, s_ref = next(it), next(it), next(it)

    hg = pl.program_id(1)
    qi = pl.program_id(2)
    dv = acc_ref.shape[1] - LANES

    gr = m_ref.shape[0] // ns

    if kind in ("slc", "win"):
        for h in range(ns):
            qs_ref[h * gr:(h + 1) * gr, :] = q_ref[:, h * HEAD_DIM:(h + 1) * HEAD_DIM]
    elif kind == "diff":
        q = q_ref[...]
        lane = lax.broadcasted_iota(I32, q.shape, 1)
        qs_ref[0:gr, :] = jnp.where(lane < DIFF_QK_DIM, q, jnp.zeros_like(q))
        qs_ref[gr:2 * gr, :] = jnp.where(lane >= DIFF_QK_DIM, q, jnp.zeros_like(q))
    else:
        qs_ref = q_ref
    m_ref[...] = jnp.full(m_ref.shape, NEG_INF, F32)
    acc_ref[...] = jnp.zeros(acc_ref.shape, F32)

    row = lax.broadcasted_iota(I32, (gr, c), 0)
    col = lax.broadcasted_iota(I32, (gr, c), 1)

    sw = min(c, ATT_C)

    def qk(k0, w):
        k0 = pl.multiple_of(k0, c)
        kc = k_ref[pl.ds(k0, w), :]
        for h in range(ns):
            qh = qs_ref[h * gr:(h + 1) * gr, :]
            for u in range(w // sw):
                s_ref[h * gr:(h + 1) * gr, u * sw:(u + 1) * sw] = _dot_nt(qh, kc[u * sw:(u + 1) * sw])

    def consume(k0, w, mode, off):
        k0 = pl.multiple_of(k0, c)
        vc = v_ref[pl.ds(k0, w), :]
        v_ext = jnp.concatenate([vc, jnp.ones((w, LANES), vc.dtype)], axis=1)
        nsub = w // sw
        mask = None
        if kind == "slc":
            selm = _dot(sel_ref[...], e_ref[:, pl.ds(k0, w)])
            if mode == "diag":
                selm = jnp.where(row >= col, selm, 0.0)
            mask = selm > 0.5
        elif mode == "diag" and kind != "mla":
            mask = row >= col
        elif mode == "edge":
            mask = col > row
        for h in range(ns):
            r0, r1 = h * gr, (h + 1) * gr
            s = s_ref[r0:r1, off:off + w]
            shift = None
            if has_bias:
                hb = hg * ns + h if kind in ("slc", "win") else hg
                if mode in ("far", "edge"):
                    shift = far_ref[hb]
                else:
                    s = s + bt_ref[h if kind in ("slc", "win") else 0, 0 if mode == "diag" else 1]
            if kind == "mla" and mode == "diag":
                mask = row + h * gr >= col
            if mask is not None:
                s = jnp.where(mask, s, NEG_INF)
            mx = jnp.max(s, axis=-1, keepdims=True)
            if shift is not None:
                mx = mx + shift
            m_prev = m_ref[r0:r1]
            m_new = jnp.maximum(m_prev, mx)
            alpha = jnp.exp(m_prev - m_new)
            sub = m_new if shift is None else m_new - shift
            p = jnp.exp(s - _rep(sub, w // LANES))
            if mask is not None:
                p = jnp.where(mask, p, 0.0)
            pb = p.astype(BF16)
            pv = _dot(pb[:, :sw], v_ext[:sw])
            for u in range(1, nsub):
                pv = pv + _dot(pb[:, u * sw:(u + 1) * sw], v_ext[u * sw:(u + 1) * sw])
            acc_ref[r0:r1] = _rep(alpha, acc_ref.shape[1] // LANES) * acc_ref[r0:r1] + pv
            m_ref[r0:r1] = m_new

    if kind == "win":
        nback = WINDOW // c
        assert nback * c == WINDOW and nback >= 2
        for back in range(nback, 0, -1):
            mode = "edge" if back == nback else ("near" if back == 1 else "far")

            @pl.when(qi - back >= 0)
            def _(back=back, mode=mode):
                qk((qi - back) * c, c)
                consume((qi - back) * c, c, mode, 0)
        qk(qi * c, c)
        consume(qi * c, c, "diag", 0)
    elif kind == "mla":
        qk(0, c)

        def far_body(j, carry):
            consume(j * c, c, "far", 0)
            qk((j + 1) * c, c)
            return carry
        lax.fori_loop(0, qi, far_body, 0)
        consume(qi * c, c, "diag", 0)
    else:
        n_pair = jnp.maximum(qi - 1, 0) // 2
        base = n_pair * (2 * c)
        qk(0, 2 * c)

        def far_body(j, carry):
            consume(j * (2 * c), 2 * c, "far", 0)
            qk((j + 1) * (2 * c), 2 * c)
            return carry
        lax.fori_loop(0, n_pair, far_body, 0)

        @pl.when(qi % 2 == 1)
        def _():
            consume(base, c, "near", 0)
            consume(base + c, c, "diag", c)

        @pl.when((qi % 2 == 0) & (qi >= 2))
        def _():
            consume(base, c, "far", 0)
            consume(base + c, c, "near", c)
            qk(base + 2 * c, 2 * c)
            consume(base + 2 * c, c, "diag", 0)

        @pl.when(qi == 0)
        def _():
            consume(0, c, "diag", 0)

    def out_stream(h):
        r0, r1 = h * gr, (h + 1) * gr
        return acc_ref[r0:r1, :dv] / jnp.maximum(acc_ref[r0:r1, dv:], 1e-30)

    if kind == "slc":
        for h in range(ns):
            o_ref[:, h * dv:(h + 1) * dv] = out_stream(h).astype(o_ref.dtype)
    elif kind == "win":
        gate = jax.nn.sigmoid(gl_ref[...])
        for h in range(ns):
            sl = slice(h * dv, (h + 1) * dv)
            o = (gate[:, 3 * h:3 * h + 1] * ocmp_ref[:, sl]
                 + gate[:, 3 * h + 1:3 * h + 2] * oslc_ref[:, sl]
                 + gate[:, 3 * h + 2:3 * h + 3] * out_stream(h))
            o_ref[:, sl] = o.astype(o_ref.dtype)
    elif kind == "diff":
        o = out_stream(0) - lam_ref[0] * out_stream(1)
        y = o * lax.rsqrt(jnp.mean(o * o, axis=-1, keepdims=True) + RMS_EPS)
        o_ref[...] = (y * g2_ref[...]).astype(o_ref.dtype)
    else:
        for h in range(ns):
            o_ref[h * gr:(h + 1) * gr, :] = out_stream(h).astype(o_ref.dtype)


def _attn_call(kind, scalars, tensors, in_specs, out_spec, out_shape, grid, *, c, ns, gr, dk, dv):
    scratch = []
    if kind != "mla":
        scratch.append(pltpu.VMEM((ns * gr, dk), BF16))
    assert dv == LANES
    sw_cols = 2 * c if kind in ("slc", "diff") else c
    scratch += [pltpu.VMEM((ns * gr, LANES), F32), pltpu.VMEM((ns * gr, dv + LANES), F32),
                pltpu.VMEM((ns * gr, sw_cols), F32)]
    assert kind not in ("slc", "diff") or (grid[2] % 2 == 0 and grid[2] >= 2)
    cfg = dict(kind=kind, c=c, ns=ns)
    return pl.pallas_call(
        functools.partial(_attn_kernel, cfg=cfg),
        grid_spec=pltpu.PrefetchScalarGridSpec(
            num_scalar_prefetch=len(scalars), grid=grid, in_specs=in_specs,
            out_specs=out_spec, scratch_shapes=scratch),
        out_shape=out_shape,
        compiler_params=_cparams(3),
        name="attn_" + kind,
    )(*scalars, *tensors)


def nsa_slc(p3, sel, expand, tiles, far, *, c=ATT_C):
    b, t, _ = p3.shape
    g, hpg, dh = NSA_KV_HEADS, NSA_HPG, HEAD_DIM
    nslc = t // SLC_BLOCK
    kb, vb = 12, 14
    in_specs = [pl.BlockSpec((None, c, hpg * dh), lambda bi, gi, qi, *_: (bi, qi, gi)),
                pl.BlockSpec((None, t, dh), lambda bi, gi, qi, *_: (bi, 0, kb + gi)),
                pl.BlockSpec((None, t, dh), lambda bi, gi, qi, *_: (bi, 0, vb + gi)),
                pl.BlockSpec((hpg, 2, c, c), lambda bi, gi, qi, *_: (gi, 0, 0, 0)),
                pl.BlockSpec((None, None, c, nslc), lambda bi, gi, qi, *_: (bi, gi, qi, 0)),
                pl.BlockSpec((nslc, t), lambda bi, gi, qi, *_: (0, 0))]
    return _attn_call(
        "slc", [far], [p3, p3, p3, tiles, sel, expand], in_specs,
        pl.BlockSpec((None, c, hpg * dh), lambda bi, gi, qi, *_: (bi, qi, gi)),
        jax.ShapeDtypeStruct((b, t, NSA_HEADS * dh), F32), (b, g, t // c),
        c=c, ns=hpg, gr=c, dk=dh, dv=dh)


def nsa_win(p3, gates3, o_cmp, o_slc, tiles, far, *, c=ATT_C):
    b, t, _ = p3.shape
    g, hpg, dh = NSA_KV_HEADS, NSA_HPG, HEAD_DIM
    kb, vb = 16, 18
    ospec = pl.BlockSpec((None, c, hpg * dh), lambda bi, gi, qi, *_: (bi, qi, gi))
    in_specs = [ospec,
                pl.BlockSpec((None, t, dh), lambda bi, gi, qi, *_: (bi, 0, kb + gi)),
                pl.BlockSpec((None, t, dh), lambda bi, gi, qi, *_: (bi, 0, vb + gi)),
                pl.BlockSpec((hpg, 2, c, c), lambda bi, gi, qi, *_: (gi, 0, 0, 0)),
                pl.BlockSpec((None, c, LANES), lambda bi, gi, qi, *_: (bi, qi, gi)),
                ospec, ospec]
    return _attn_call(
        "win", [far], [p3, p3, p3, tiles, gates3, o_cmp, o_slc], in_specs, ospec,
        jax.ShapeDtypeStruct((b, t, NSA_HEADS * dh), BF16), (b, g, t // c),
        c=c, ns=hpg, gr=c, dk=dh, dv=dh)


def diff_attn(p3, tiles, far, lam, g2, *, c=ATT_C):
    b, t, _ = p3.shape
    dh = HEAD_DIM
    qb, kb, vb = 20, 28, 36
    in_specs = [pl.BlockSpec((None, c, dh), lambda bi, hi, qi, *_: (bi, qi, qb + hi)),
                pl.BlockSpec((None, t, dh), lambda bi, hi, qi, *_: (bi, 0, kb + hi)),
                pl.BlockSpec((None, t, dh), lambda bi, hi, qi, *_: (bi, 0, vb + hi)),
                pl.BlockSpec((1, 2, c, c), lambda bi, hi, qi, *_: (hi, 0, 0, 0)),
                pl.BlockSpec((1, dh), lambda bi, hi, qi, *_: (0, 0))]
    return _attn_call(
        "diff", [far, lam], [p3, p3, p3, tiles, g2], in_specs,
        pl.BlockSpec((None, c, dh), lambda bi, hi, qi, *_: (bi, qi, hi)),
        jax.ShapeDtypeStruct((b, t, DIFF_HEADS * dh), BF16), (b, DIFF_HEADS, t // c),
        c=c, ns=2, gr=c, dk=dh, dv=dh)


def mla_attn(qf, kf, kv3, *, c=MLA_C):
    b, t, _ = qf.shape
    dk = 2 * HEAD_DIM
    c = min(c, t)
    in_specs = [pl.BlockSpec((None, c, dk), lambda bi, hi, qi: (bi, qi, hi)),
                pl.BlockSpec((None, t, dk), lambda bi, hi, qi: (bi, 0, hi)),
                pl.BlockSpec((None, t, MLA_V), lambda bi, hi, qi: (bi, 0, MLA_HEADS + hi))]
    return _attn_call(
        "mla", [], [qf, kf, kv3], in_specs,
        pl.BlockSpec((None, c, MLA_V), lambda bi, hi, qi: (bi, qi, hi)),
        jax.ShapeDtypeStruct((b, t, MLA_HEADS * MLA_V), BF16), (b, MLA_HEADS, t // c),
        c=c, ns=2, gr=c // 2, dk=dk, dv=MLA_V)


def _mla_assemble_kernel(q_ref, kv_ref, c_ref, cos_ref, sin_ref, qf_ref, kf_ref):
    h16 = MLA_HEADS
    cos = cos_ref[...]
    sin = sin_ref[...]
    lane = lax.broadcasted_iota(I32, cos.shape, 1)
    first = lane < MLA_ROPE
    kpe = c_ref[:, 0:LANES] * cos + c_ref[:, LANES:2 * LANES] * sin
    for j in range(h16 // 2):
        pe = q_ref[:, (h16 + j) * LANES:(h16 + j + 1) * LANES]
        pep = q_ref[:, (h16 + h16 // 2 + j) * LANES:(h16 + h16 // 2 + j + 1) * LANES]
        qpe = pe * cos + pep * sin
        for par in range(2):
            h = 2 * j + par
            keep = first if par == 0 else jnp.logical_not(first)
            qf_ref[:, (2 * h) * LANES:(2 * h + 1) * LANES] = q_ref[:, h * LANES:(h + 1) * LANES].astype(BF16)
            qf_ref[:, (2 * h + 1) * LANES:(2 * h + 2) * LANES] = jnp.where(keep, qpe, 0.0).astype(BF16)
            kf_ref[:, (2 * h) * LANES:(2 * h + 1) * LANES] = kv_ref[:, h * LANES:(h + 1) * LANES].astype(BF16)
            kf_ref[:, (2 * h + 1) * LANES:(2 * h + 2) * LANES] = jnp.where(keep, kpe, 0.0).astype(BF16)


def mla_assemble(qraw, kvraw, craw, cos2, sin2, seq, *, tm=256):
    n = qraw.shape[0]
    tm = min(tm, seq)
    nt = seq // tm
    wq = qraw.shape[1]
    return pl.pallas_call(
        _mla_assemble_kernel,
        grid=(n // tm,),
        in_specs=[pl.BlockSpec((tm, wq), lambda i: (i, 0)),
                  pl.BlockSpec((tm, MLA_HEADS * LANES), lambda i: (i, 0)),
                  pl.BlockSpec((tm, 2 * LANES), lambda i: (i, 4)),
                  pl.BlockSpec((tm, LANES), lambda i: (i % nt, 0)),
                  pl.BlockSpec((tm, LANES), lambda i: (i % nt, 0))],
        out_specs=[pl.BlockSpec((tm, 2 * MLA_HEADS * LANES), lambda i: (i, 0)),
                   pl.BlockSpec((tm, 2 * MLA_HEADS * LANES), lambda i: (i, 0))],
        out_shape=[jax.ShapeDtypeStruct((n, 2 * MLA_HEADS * LANES), BF16)] * 2,
        compiler_params=_cparams(1),
        name="mla_assemble",
    )(qraw, kvraw, craw, cos2, sin2)


def _mem_attn_kernel(q_ref, kv_ref, o_ref):
    hd = MEM_HEAD_DIM
    voff = MEM_HEADS * hd
    for h in range(MEM_HEADS):
        sl = slice(h * hd, (h + 1) * hd)
        s = _dot_nt(q_ref[:, sl], kv_ref[:, sl])
        m = jnp.max(s, axis=-1, keepdims=True)
        p = jnp.exp(s - m)
        p = p / jnp.sum(p, axis=-1, keepdims=True)
        o_ref[:, sl] = _dot(p.astype(BF16), kv_ref[:, voff + h * hd:voff + (h + 1) * hd]).astype(o_ref.dtype)


def mem_attn(q3, kv3, *, tq=512):
    b, t, w = q3.shape
    m = kv3.shape[1]
    tq = min(tq, t)
    return pl.pallas_call(
        _mem_attn_kernel,
        grid=(b, t // tq),
        in_specs=[pl.BlockSpec((None, tq, w), lambda bi, qi: (bi, qi, 0)),
                  pl.BlockSpec((None, m, 2 * w), lambda bi, qi: (bi, 0, 0))],
        out_specs=pl.BlockSpec((None, tq, w), lambda bi, qi: (bi, qi, 0)),
        out_shape=jax.ShapeDtypeStruct((b, t, w), BF16),
        compiler_params=_cparams(2),
        name="mem_attn",
    )(q3, kv3)


def _router_kernel(h_ref, g_ref, rw_ref, rb_ref, hn_ref, ro_ref):
    x = h_ref[...]
    hn = x * lax.rsqrt(jnp.mean(x * x, axis=-1, keepdims=True) + RMS_EPS) * g_ref[...]
    hn_ref[...] = hn
    logits = jnp.dot(hn, rw_ref[...], precision=HIGHEST, preferred_element_type=F32) + rb_ref[...]
    lane = lax.broadcasted_iota(I32, logits.shape, 1)
    lg = jnp.where(lane < N_EXPERTS, logits, -jnp.inf)
    v1 = jnp.max(lg, axis=-1, keepdims=True)
    i1 = jnp.min(jnp.where(lg == v1, lane, LANES), axis=-1, keepdims=True)
    lg2 = jnp.where(lane == i1, -jnp.inf, lg)
    v2 = jnp.max(lg2, axis=-1, keepdims=True)
    i2 = jnp.min(jnp.where(lg2 == v2, lane, LANES), axis=-1, keepdims=True)
    e2 = jnp.exp(v2 - v1)
    w1 = 1.0 / (1.0 + e2)
    w2 = e2 / (1.0 + e2)
    ro_ref[...] = jnp.where(lane == 0, i1.astype(F32),
                            jnp.where(lane == 1, i2.astype(F32),
                                      jnp.where(lane == 2, w1, jnp.where(lane == 3, w2, 0.0))))


def moe_router(h, g, rw, rb, *, tm=256):
    n, d = h.shape
    rwp = jnp.zeros((d, LANES), F32).at[:, :N_EXPERTS].set(rw.astype(F32))
    rbp = jnp.zeros((1, LANES), F32).at[0, :N_EXPERTS].set(rb.astype(F32))
    return pl.pallas_call(
        _router_kernel,
        grid=(n // tm,),
        in_specs=[pl.BlockSpec((tm, d), lambda i: (i, 0)),
                  pl.BlockSpec((1, d), lambda i: (0, 0)),
                  pl.BlockSpec((d, LANES), lambda i: (0, 0)),
                  pl.BlockSpec((1, LANES), lambda i: (0, 0))],
        out_specs=[pl.BlockSpec((tm, d), lambda i: (i, 0)),
                   pl.BlockSpec((tm, LANES), lambda i: (i, 0))],
        out_shape=[jax.ShapeDtypeStruct((n, d), F32), jax.ShapeDtypeStruct((n, LANES), F32)],
        compiler_params=_cparams(1),
        name="moe_router",
    )(h, g.reshape(1, d).astype(F32), rwp, rbp)


def _row_copy(src, dst, sem):
    return pltpu.make_async_copy(src, dst, sem)


def _dispatch_kernel(d1_ref, d2_ref, x_ref, xs_in_ref, xs_ref, sem):
    del xs_in_ref
    tm = x_ref.shape[0]
    base = pl.program_id(0) * tm

    def issue(r, carry):
        src = x_ref.at[pl.ds(r, 1), :]
        _row_copy(src, xs_ref.at[pl.ds(d1_ref[base + r], 1), :], sem).start()
        _row_copy(src, xs_ref.at[pl.ds(d2_ref[base + r], 1), :], sem).start()
        return carry

    def drain(r, carry):
        src = x_ref.at[pl.ds(r, 1), :]
        _row_copy(src, xs_ref.at[pl.ds(0, 1), :], sem).wait()
        _row_copy(src, xs_ref.at[pl.ds(0, 1), :], sem).wait()
        return carry

    lax.fori_loop(0, tm, issue, 0)
    lax.fori_loop(0, tm, drain, 0)


def moe_dispatch(hn, d1, d2, rows, *, tm=256):
    n, d = hn.shape
    return pl.pallas_call(
        _dispatch_kernel,
        grid_spec=pltpu.PrefetchScalarGridSpec(
            num_scalar_prefetch=2, grid=(n // tm,),
            in_specs=[pl.BlockSpec((tm, d), lambda i, *_: (i, 0)),
                      pl.BlockSpec(memory_space=pl.ANY)],
            out_specs=pl.BlockSpec(memory_space=pl.ANY),
            scratch_shapes=[pltpu.SemaphoreType.DMA(())]),
        out_shape=jax.ShapeDtypeStruct((rows, d), hn.dtype),
        input_output_aliases={3: 0},
        compiler_params=_cparams(1),
        name="moe_dispatch",
    )(d1, d2, hn, jnp.zeros((rows, d), hn.dtype))


def _combine_kernel(d1_ref, d2_ref, ys_ref, h_ref, ro_ref, o_ref, b1_ref, b2_ref, sem):
    tm = h_ref.shape[0]
    base = pl.program_id(0) * tm

    def issue(r, carry):
        _row_copy(ys_ref.at[pl.ds(d1_ref[base + r], 1), :], b1_ref.at[pl.ds(r, 1), :], sem).start()
        _row_copy(ys_ref.at[pl.ds(d2_ref[base + r], 1), :], b2_ref.at[pl.ds(r, 1), :], sem).start()
        return carry

    def drain(r, carry):
        _row_copy(ys_ref.at[pl.ds(0, 1), :], b1_ref.at[pl.ds(r, 1), :], sem).wait()
        _row_copy(ys_ref.at[pl.ds(0, 1), :], b2_ref.at[pl.ds(r, 1), :], sem).wait()
        return carry

    lax.fori_loop(0, tm, issue, 0)
    lax.fori_loop(0, tm, drain, 0)
    ro = ro_ref[...]
    o_ref[...] = h_ref[...] + ro[:, 2:3] * b1_ref[...] + ro[:, 3:4] * b2_ref[...]


def moe_combine(ys, h, ro, d1, d2, *, tm=256):
    n, d = h.shape
    return pl.pallas_call(
        _combine_kernel,
        grid_spec=pltpu.PrefetchScalarGridSpec(
            num_scalar_prefetch=2, grid=(n // tm,),
            in_specs=[pl.BlockSpec(memory_space=pl.ANY),
                      pl.BlockSpec((tm, d), lambda i, *_: (i, 0)),
                      pl.BlockSpec((tm, LANES), lambda i, *_: (i, 0))],
            out_specs=pl.BlockSpec((tm, d), lambda i, *_: (i, 0)),
            scratch_shapes=[pltpu.VMEM((tm, d), F32), pltpu.VMEM((tm, d), F32),
                            pltpu.SemaphoreType.DMA(())]),
        out_shape=jax.ShapeDtypeStruct((n, d), F32),
        compiler_params=_cparams(1),
        name="moe_combine",
    )(d1, d2, ys, h, ro)


def moe_block(h, g, rw, rb, w_gu, w_down, layer, *, tm_e=EXPERT_TM):
    n, d = h.shape
    tm_e = min(tm_e, n)
    hn, ro = moe_router(h, g, rw, rb)
    e1 = ro[:, 0].astype(I32)
    e2 = ro[:, 1].astype(I32)
    ind = (jax.nn.one_hot(e1, N_EXPERTS, dtype=I32) + jax.nn.one_hot(e2, N_EXPERTS, dtype=I32))
    csum = jnp.cumsum(ind, axis=0)
    rank = csum - ind
    cnt = csum[-1]
    pcnt = (cnt + tm_e - 1) // tm_e * tm_e
    ends = jnp.cumsum(pcnt)
    off = ends - pcnt
    d1 = (off[e1] + jnp.take_along_axis(rank, e1[:, None], axis=1)[:, 0]).astype(I32)
    d2 = (off[e2] + jnp.take_along_axis(rank, e2[:, None], axis=1)[:, 0]).astype(I32)
    rows = 2 * n + N_EXPERTS * tm_e
    tile_start = jnp.arange(rows // tm_e, dtype=I32) * tm_e
    tile_e = jnp.minimum(jnp.sum((tile_start[:, None] >= ends[None, :]).astype(I32), axis=1),
                         N_EXPERTS - 1).astype(I32)
    xs = moe_dispatch(hn, d1, d2, rows)
    tile_w = tile_e + layer * N_EXPERTS
    n_valid = (ends[-1] // tm_e).astype(I32).reshape(1)
    act = gmm(xs, w_gu, tile_w, n_valid, tm=tm_e, out_dtype=BF16, swiglu=True)
    ys = gmm(act, w_down, tile_w, n_valid, tm=tm_e, out_dtype=F32)
    return moe_combine(ys, h, ro, d1, d2)


def _mem_block(h, mem2, b, t, i, g_ca, g_mem, wq, wkv, wo):
    n, d = h.shape
    hn = rmsnorm(h, g_ca[i], BF16)
    memn = rmsnorm(mem2, g_mem[i], BF16)
    q = mm(hn, (wq[i] * MEM_HEAD_DIM ** -0.5).astype(BF16), out_dtype=BF16)
    kv = mm(memn, wkv, group=i, out_dtype=BF16)
    o = mem_attn(q.reshape(b, t, -1), kv.reshape(b, mem2.shape[0] // b, -1))
    return mm(o.reshape(n, -1), wo, group=i, out_dtype=F32, res=h)


def _even_layer(h, b, t, li, lam_init, p, tabs):
    n, d = h.shape
    w_in = p["even_w_in"][li]
    nq, nkv = NSA_HEADS * HEAD_DIM, NSA_KV_HEADS * HEAD_DIM
    g0 = nq + 6 * nkv
    dq0 = g0 + 3 * NSA_HEADS
    dqw = DIFF_HEADS * 2 * DIFF_QK_DIM
    w_main = jnp.concatenate([w_in[:, :nq] * HEAD_DIM ** -0.5, w_in[:, nq:g0],
                              w_in[:, dq0:dq0 + dqw] * DIFF_QK_DIM ** -0.5, w_in[:, dq0 + dqw:]],
                             axis=1).astype(BF16)
    w_gate = jnp.zeros((d, NSA_KV_HEADS * LANES), F32)
    for g in range(NSA_KV_HEADS):
        w_gate = w_gate.at[:, g * LANES:g * LANES + 3 * NSA_HPG].set(
            w_in[:, g0 + g * 3 * NSA_HPG:g0 + (g + 1) * 3 * NSA_HPG])
    xn = rmsnorm(h, p["norm_mix"][2 * li], BF16)
    proj = mm(xn, w_main, out_dtype=BF16)
    gates = mm(xn, w_gate.astype(BF16), out_dtype=F32)
    p3 = proj.reshape(b, t, -1)

    kvs = p3[:, :, nq:nq + 2 * nkv].reshape(b, t, 2, NSA_KV_HEADS, HEAD_DIM)
    acat = kvs.transpose(2, 0, 3, 1, 4).reshape(2, b, NSA_KV_HEADS, t // CMP_STRIDE, CMP_STRIDE * HEAD_DIM)
    w1 = p["nsa_cmp_w1"][li]
    half = CMP_STRIDE * HEAD_DIM
    w1cat = jnp.concatenate([w1[:, :half], w1[:, half:]], axis=2).astype(BF16)
    pos = p["nsa_cmp_pos"][li].reshape(2, 2, half)
    pos2 = jnp.zeros((2, 8, half), F32).at[:, :2].set(pos).astype(BF16)
    kvc = nsa_compress(acat, w1cat, p["nsa_cmp_w2"][li].astype(BF16), pos2)
    o_cmp, sel = nsa_cmp_select(p3, kvc, tabs["cbias"], tabs["overlap"])
    o_slc = nsa_slc(p3, sel, tabs["expand"], tabs["tiles_a"], tabs["far_a"])
    o_a = nsa_win(p3, gates.reshape(b, t, -1), o_cmp, o_slc, tabs["tiles_a"], tabs["far_a"])

    lq1, lk1, lq2, lk2 = p["diff_lambda"][li].astype(F32)
    lam = jnp.exp(jnp.sum(lq1 * lk1)) - jnp.exp(jnp.sum(lq2 * lk2)) + lam_init
    g2 = (p["diff_subln"][li].astype(F32) * (1.0 - lam_init)).reshape(1, HEAD_DIM)
    o_b = diff_attn(p3, tabs["tiles_b"], tabs["far_b"], lam.reshape(1), g2)

    mix_in = jnp.concatenate([o_a, o_b], axis=-1).reshape(n, -1)
    h = mm(mix_in, p["even_w_out"], group=li, out_dtype=F32, res=h)
    return h


def _odd_layer(h, b, t, li, p, tabs):
    n, d = h.shape
    hh, nope, rp = MLA_HEADS, MLA_NOPE, MLA_ROPE
    scale = (nope + rp) ** -0.5

    def rot(w):
        w = w.reshape(w.shape[0], -1, 2, rp // 2)
        return jnp.stack([-w[:, :, 1], w[:, :, 0]], axis=2).reshape(w.shape[0], -1)

    w_in = p["odd_w_in"][li]
    kpe_w = w_in[:, MLA_Q_RANK + MLA_KV_RANK:]
    w_in_ext = jnp.concatenate([w_in[:, :MLA_Q_RANK + MLA_KV_RANK], kpe_w, kpe_w, rot(kpe_w), rot(kpe_w)],
                               axis=1).astype(BF16)
    wq = (p["mla_w_q_up"][li] * scale).reshape(MLA_Q_RANK, hh, nope + rp)
    wq_pe = wq[:, :, nope:].reshape(MLA_Q_RANK, hh * rp)
    wq_ext = jnp.concatenate([wq[:, :, :nope].reshape(MLA_Q_RANK, hh * nope), wq_pe, rot(wq_pe)],
                             axis=1).astype(BF16)
    wkv = p["mla_w_kv_up"][li].reshape(MLA_KV_RANK, hh, nope + MLA_V)
    wkv_ext = jnp.concatenate([wkv[:, :, :nope].reshape(MLA_KV_RANK, -1),
                               wkv[:, :, nope:].reshape(MLA_KV_RANK, -1)], axis=1).astype(BF16)

    xn = rmsnorm(h, p["norm_mix"][2 * li + 1], BF16)
    craw = mm(xn, w_in_ext, out_dtype=F32)
    cqn = rmsnorm(craw, p["mla_q_norm"][li], BF16, width=MLA_Q_RANK, col_block=0)
    ckvn = rmsnorm(craw, p["mla_kv_norm"][li], BF16, width=MLA_KV_RANK, col_block=1)
    qraw = mm(cqn, wq_ext, out_dtype=F32)
    kvraw = mm(ckvn, wkv_ext, out_dtype=BF16)
    qf, kf = mla_assemble(qraw, kvraw, craw, tabs["cos2"], tabs["sin2"], t)
    o = mla_attn(qf.reshape(b, t, -1), kf.reshape(b, t, -1), kvraw.reshape(b, t, -1))
    return mm(o.reshape(n, -1), p["odd_w_out"], group=li, out_dtype=F32, res=h)


def _tables(rel_bias, t):
    bias_a, bias_b = rel_bias[:, :NSA_HEADS], rel_bias[:, NSA_HEADS:]
    tiles_a, far_a = _bias_tiles(bias_a, ATT_C)
    tiles_b, far_b = _bias_tiles(bias_b, ATT_C)
    ncb = t // CMP_STRIDE
    nslc = t // SLC_BLOCK
    tt = jnp.arange(t)
    blk_start = jnp.arange(ncb) * CMP_STRIDE
    cbias = _bias_lookup(tt[:, None] - (blk_start + CMP_BLOCK - 1)[None, :], bias_a)
    jj = jnp.arange(nslc)
    n_cmp = (t - CMP_BLOCK) // CMP_STRIDE + 1
    overlap = ((blk_start[:, None] < (jj[None, :] + 1) * SLC_BLOCK)
               & (blk_start[:, None] + CMP_BLOCK > jj[None, :] * SLC_BLOCK)
               & (jnp.arange(ncb)[:, None] < n_cmp)).astype(F32)
    expand = (jj[:, None] == (tt[None, :] // SLC_BLOCK)).astype(BF16)
    inv_freq = ROPE_THETA ** (-jnp.arange(0, MLA_ROPE, 2, dtype=F32) / MLA_ROPE)
    ang = jnp.arange(t, dtype=F32)[:, None] * inv_freq
    cos2 = jnp.tile(jnp.cos(ang), (1, 2 * LANES // MLA_ROPE))
    sin2 = jnp.tile(jnp.sin(ang), (1, 2 * LANES // MLA_ROPE))
    return dict(tiles_a=tiles_a, far_a=far_a, tiles_b=tiles_b, far_b=far_b, cbias=cbias,
                overlap=overlap, expand=expand, cos2=cos2, sin2=sin2)


def kernel(x, mem, rel_bias, norm_mix, norm_mem, norm_ca, norm_ffn, norm_final, even_w_in, even_w_out, nsa_cmp_pos, nsa_cmp_w1, nsa_cmp_w2, diff_lambda, diff_subln, ffn_w_gu, ffn_w_down, odd_w_in, mla_q_norm, mla_kv_norm, mla_w_q_up, mla_w_kv_up, odd_w_out, router_w, router_b, moe_w_gu, moe_w_down, ca_wq, ca_wkv, ca_wo):
    p = dict(norm_mix=norm_mix, even_w_in=even_w_in, even_w_out=even_w_out, nsa_cmp_pos=nsa_cmp_pos,
             nsa_cmp_w1=nsa_cmp_w1, nsa_cmp_w2=nsa_cmp_w2, diff_lambda=diff_lambda, diff_subln=diff_subln,
             odd_w_in=odd_w_in, mla_q_norm=mla_q_norm, mla_kv_norm=mla_kv_norm, mla_w_q_up=mla_w_q_up,
             mla_w_kv_up=mla_w_kv_up, odd_w_out=odd_w_out)
    b, t, d = x.shape
    depth = norm_mix.shape[0]
    tabs = _tables(rel_bias, t)
    h = x.reshape(b * t, d).astype(F32)
    mem2 = mem.reshape(-1, d).astype(F32)
    for i in range(depth):
        li = i // 2
        if i % 2 == 0:
            h = _even_layer(h, b, t, li, 0.8 - 0.6 * math.exp(-0.3 * i), p, tabs)
        else:
            h = _odd_layer(h, b, t, li, p, tabs)
        h = _mem_block(h, mem2, b, t, i, norm_ca, norm_mem, ca_wq, ca_wkv, ca_wo)
        if i % 2 == 0:
            hn = rmsnorm(h, norm_ffn[i], BF16)
            act = mm(hn, ffn_w_gu, group=li, out_dtype=BF16, swiglu=True)
            h = mm(act, ffn_w_down, group=li, out_dtype=F32, res=h)
        else:
            h = moe_block(h, norm_ffn[i], router_w[li], router_b[li],
                          moe_w_gu.reshape((-1,) + moe_w_gu.shape[2:]),
                          moe_w_down.reshape((-1,) + moe_w_down.shape[2:]), li)
    return rmsnorm(h, norm_final, x.dtype).reshape(b, t, d)
```

```python
import functools
import math

import jax
import jax.numpy as jnp
from jax import lax
from jax.experimental import pallas as pl
from jax.experimental.pallas import tpu as pltpu

F32 = jnp.float32
BF16 = jnp.bfloat16
I32 = jnp.int32

HEAD_DIM = 128
NSA_HEADS = 8
NSA_KV_HEADS = 2
NSA_HPG = NSA_HEADS // NSA_KV_HEADS
CMP_BLOCK = 32
CMP_STRIDE = 16
SLC_BLOCK = 64
SLC_TOPK = 16
WINDOW = 512
DIFF_HEADS = 8
DIFF_QK_DIM = 64
MLA_HEADS = 16
MLA_Q_RANK = 512
MLA_KV_RANK = 512
MLA_NOPE = 128
MLA_ROPE = 64
MLA_V = 128
ROPE_THETA = 10000.0
MEM_HEADS = 4
MEM_HEAD_DIM = 128
N_EXPERTS = 8
NUM_BUCKETS = 32
MAX_DISTANCE = 128
RMS_EPS = 1e-6
NEG_INF = -1e30
BIG = 1e30

LANES = 128
MXU_COLS = 256
GMM_VMEM_BUDGET = 46 * 1024 * 1024
VMEM_LIMIT = 56 * 1024 * 1024
ATT_C = 256
MLA_C = 512
EXPERT_TM = 512
HIGHEST = lax.Precision.HIGHEST


def _cparams(n_axes):
    return pltpu.CompilerParams(dimension_semantics=("arbitrary",) * n_axes,
                                vmem_limit_bytes=VMEM_LIMIT)


def _dot(a, b):
    return jnp.dot(a, b, preferred_element_type=F32)


def _rep(x, n):
    return x if n == 1 else jnp.concatenate([x] * n, axis=1)


def _rep_cat(xs):
    return xs[0] if len(xs) == 1 else jnp.concatenate(xs, axis=1)


def _dot_nt(a, b):
    return lax.dot_general(a, b, (((1,), (1,)), ((), ())), preferred_element_type=F32)


def _rms_kernel(x_ref, g_ref, o_ref):
    x = x_ref[...].astype(F32)
    y = x * lax.rsqrt(jnp.mean(x * x, axis=-1, keepdims=True) + RMS_EPS)
    o_ref[...] = (y * g_ref[...]).astype(o_ref.dtype)


def rmsnorm(x, g, out_dtype, width=None, col_block=0, tm=512):
    m = x.shape[0]
    width = width or x.shape[1]
    tm = min(tm, m)
    return pl.pallas_call(
        _rms_kernel,
        grid=(m // tm,),
        in_specs=[pl.BlockSpec((tm, width), lambda i: (i, col_block)),
                  pl.BlockSpec((1, width), lambda i: (0, 0))],
        out_specs=pl.BlockSpec((tm, width), lambda i: (i, 0)),
        out_shape=jax.ShapeDtypeStruct((m, width), out_dtype),
        compiler_params=_cparams(1),
        name="rmsnorm",
    )(x, g.reshape(1, width).astype(F32))


def _gmm_kernel(te_ref, nv_ref, x_ref, w_ref, *rest, swiglu, has_res, cast_w):
    nw = 2 if swiglu else 1
    w_refs = (w_ref,) + tuple(rest[:nw - 1])
    res_ref = rest[nw - 1] if has_res else None
    o_ref = rest[nw - 1 + int(has_res)]
    i = pl.program_id(1)
    if cast_w:
        wb_refs = rest[nw + int(has_res):]
        prev = te_ref[jnp.maximum(i - 1, 0)]

        @pl.when((i == 0) | (te_ref[i] != prev))
        def _():
            for src, dst in zip(w_refs, wb_refs):
                dst[...] = src[...].astype(BF16)
        w_refs = wb_refs

    @pl.when(i < nv_ref[0])
    def _():
        x = x_ref[...].astype(BF16)
        acc = _dot(x, w_refs[0][...])
        if swiglu:
            acc = acc * jax.nn.sigmoid(acc) * _dot(x, w_refs[1][...])
        if has_res:
            acc = acc + res_ref[...]
        o_ref[...] = acc.astype(o_ref.dtype)

    @pl.when(i >= nv_ref[0])
    def _():
        o_ref[...] = jnp.zeros(o_ref.shape, o_ref.dtype)


def _gmm_tn(k, n, tm, x_bytes, w_bytes, out_bytes, n_w, has_res):
    best = None
    for tn in range(LANES, n + 1, LANES):
        if n % tn or (tn % MXU_COLS and tn != n):
            continue
        fp = (n_w * k * tn * w_bytes * 2 + (n_w * k * tn * 2 if w_bytes == 4 else 0)
              + tm * k * x_bytes * 2 + tm * tn * out_bytes * 2 + tm * tn * 4 * n_w
              + (tm * tn * 4 * 2 if has_res else 0))
        if fp <= GMM_VMEM_BUDGET:
            best = tn
    assert best is not None, (k, n, tm)
    return best


def gmm(x, w, tile_e, n_valid, *, tm, out_dtype, swiglu=False, res=None):
    r, k = x.shape
    n = w.shape[2] // 2 if swiglu else w.shape[2]
    n_w = 2 if swiglu else 1
    cast_w = w.dtype != BF16
    tn = _gmm_tn(k, n, tm, x.dtype.itemsize, w.dtype.itemsize, jnp.dtype(out_dtype).itemsize, n_w,
                 res is not None)
    nj = n // tn
    in_specs = [pl.BlockSpec((tm, k), lambda j, i, te, nv: (jnp.minimum(i, nv[0] - 1), 0)),
                pl.BlockSpec((None, k, tn), lambda j, i, te, nv: (te[i], 0, j))]
    args = [x, w]
    if swiglu:
        in_specs.append(pl.BlockSpec((None, k, tn), lambda j, i, te, nv: (te[i], 0, j + nj)))
        args.append(w)
    if res is not None:
        in_specs.append(pl.BlockSpec((tm, tn), lambda j, i, te, nv: (i, j)))
        args.append(res)
    scratch = [pltpu.VMEM((k, tn), BF16)] * (n_w if cast_w else 0)
    return pl.pallas_call(
        functools.partial(_gmm_kernel, swiglu=swiglu, has_res=res is not None, cast_w=cast_w),
        grid_spec=pltpu.PrefetchScalarGridSpec(
            num_scalar_prefetch=2,
            grid=(nj, r // tm),
            in_specs=in_specs,
            out_specs=pl.BlockSpec((tm, tn), lambda j, i, te, nv: (i, j)),
            scratch_shapes=scratch),
        out_shape=jax.ShapeDtypeStruct((r, n), out_dtype),
        compiler_params=_cparams(2),
        name="gmm_swiglu" if swiglu else "gmm",
    )(tile_e, n_valid, *args)


def mm(x, w, *, out_dtype, tm=512, swiglu=False, res=None, group=0):
    m = x.shape[0]
    tm = min(tm, m)
    if w.ndim == 2:
        w = w[None]
    return gmm(x, w, jnp.full((m // tm,), group, I32), jnp.full((1,), m // tm, I32), tm=tm,
               out_dtype=out_dtype, swiglu=swiglu, res=res)


def _t5_bucket(dist):
    n = jnp.maximum(dist, 0)
    max_exact = NUM_BUCKETS // 2
    nf = jnp.maximum(n, 1).astype(F32)
    large = max_exact + (jnp.log(nf / max_exact) / math.log(MAX_DISTANCE / max_exact)
                         * (NUM_BUCKETS - max_exact)).astype(I32)
    large = jnp.minimum(large, NUM_BUCKETS - 1)
    return jnp.where(n < max_exact, n, large)


def _bias_lookup(dist, tbl):
    oh = jax.nn.one_hot(_t5_bucket(dist), NUM_BUCKETS, dtype=F32)
    return jnp.einsum("...k,kh->h...", oh, tbl.astype(F32), precision=HIGHEST)


def _bias_tiles(tbl, c):
    i = jnp.arange(c)[:, None]
    j = jnp.arange(c)[None, :]
    tiles = jnp.stack([_bias_lookup(i - j, tbl), _bias_lookup(c + i - j, tbl)], axis=1)
    far = _bias_lookup(jnp.asarray(2 * c), tbl)
    assert c + 1 >= MAX_DISTANCE
    return tiles, far


def _compress_kernel(a_ref, w1_ref, w2_ref, pos_ref, o_ref):
    nblk = a_ref.shape[0]
    pq = _dot(a_ref[...], w1_ref[...])
    r = _dot(pos_ref[...], w1_ref[...])
    c = r[0:1, :HEAD_DIM] + r[1:2, HEAD_DIM:]
    hp = pq[:, :HEAD_DIM] + pltpu.roll(pq[:, HEAD_DIM:], nblk - 1, 0) + c
    hdn = jax.nn.gelu(hp)
    o_ref[...] = _dot(hdn.astype(BF16), w2_ref[...]).astype(o_ref.dtype)


def nsa_compress(acat, w1cat, w2, pos2):
    _, b, g, nblk, wid = acat.shape
    return pl.pallas_call(
        _compress_kernel,
        grid=(2, b, g),
        in_specs=[pl.BlockSpec((None, None, None, nblk, wid), lambda s, bi, gi: (s, bi, gi, 0, 0)),
                  pl.BlockSpec((None, wid, 2 * HEAD_DIM), lambda s, bi, gi: (s, 0, 0)),
                  pl.BlockSpec((None, HEAD_DIM, HEAD_DIM), lambda s, bi, gi: (s, 0, 0)),
                  pl.BlockSpec((None, 8, wid), lambda s, bi, gi: (s, 0, 0))],
        out_specs=pl.BlockSpec((None, None, None, nblk, HEAD_DIM), lambda s, bi, gi: (s, bi, gi, 0, 0)),
        out_shape=jax.ShapeDtypeStruct((2, b, g, nblk, HEAD_DIM), BF16),
        compiler_params=_cparams(3),
        name="nsa_compress",
    )(acat, w1cat, w2, pos2)


def _nsa_cmp_kernel(q_ref, kc_ref, vc_ref, cb_ref, ov_ref, o_ref, sel_ref, sc_ref, *, tq, topk):
    qi = pl.program_id(2)
    ncb = kc_ref.shape[0]
    nslc = ov_ref.shape[0]
    t = qi * tq + lax.broadcasted_iota(I32, (tq, ncb), 0)
    n = lax.broadcasted_iota(I32, (tq, ncb), 1)
    mask = n * CMP_STRIDE + (CMP_BLOCK - 1) <= t
    kc = kc_ref[...]
    vc = vc_ref[...]
    psum = jnp.zeros((tq, ncb), F32)
    for h in range(NSA_HPG):
        qh = q_ref[:, h * HEAD_DIM:(h + 1) * HEAD_DIM]
        s = jnp.where(mask, _dot_nt(qh, kc) + cb_ref[h], NEG_INF)
        m = jnp.max(s, axis=-1, keepdims=True)
        p = jnp.where(mask, jnp.exp(s - m), 0.0)
        p = p / jnp.maximum(jnp.sum(p, axis=-1, keepdims=True), 1e-30)
        o_ref[:, h * HEAD_DIM:(h + 1) * HEAD_DIM] = _dot(p.astype(BF16), vc).astype(o_ref.dtype)
        psum = psum + p
    imp = lax.dot_general(ov_ref[...], psum, (((1,), (1,)), ((), ())), precision=HIGHEST,
                          preferred_element_type=F32)
    tt = qi * tq + lax.broadcasted_iota(I32, (nslc, tq), 1)
    j = lax.broadcasted_iota(I32, (nslc, tq), 0)
    cur = tt // SLC_BLOCK
    forced = (j == 0) | (j == cur) | (j == cur - 1)
    visible = j * SLC_BLOCK <= tt
    sc_ref[...] = jnp.where(forced, BIG, jnp.where(visible, imp, NEG_INF))
    score = sc_ref[...]
    rank = jnp.zeros((nslc, tq), F32)
    for i in range(nslc):
        other = sc_ref[i:i + 1, :]
        tie = jnp.where(j > i, 1.0, 0.0)
        rank = rank + jnp.where(other > score, 1.0, jnp.where(other == score, tie, 0.0))
    neg = jnp.where(rank < topk, 0.0, NEG_INF)
    neg = jnp.concatenate([neg, jnp.zeros((LANES - nslc, tq), F32)], axis=0)
    sel_ref[...] = neg.T.astype(sel_ref.dtype)


def nsa_cmp_select(p3, kvc, cbias, overlap, *, tq=ATT_C):
    b, t, _ = p3.shape
    g = NSA_KV_HEADS
    ncb = kvc.shape[3]
    nslc = t // SLC_BLOCK
    assert nslc <= LANES and nslc % 8 == 0
    gw = NSA_HPG * HEAD_DIM
    return pl.pallas_call(
        functools.partial(_nsa_cmp_kernel, tq=tq, topk=min(SLC_TOPK, nslc)),
        grid=(b, g, t // tq),
        in_specs=[pl.BlockSpec((None, tq, gw), lambda bi, gi, qi: (bi, qi, gi)),
                  pl.BlockSpec((None, None, None, ncb, HEAD_DIM), lambda bi, gi, qi: (0, bi, gi, 0, 0)),
                  pl.BlockSpec((None, None, None, ncb, HEAD_DIM), lambda bi, gi, qi: (1, bi, gi, 0, 0)),
                  pl.BlockSpec((NSA_HPG, tq, ncb), lambda bi, gi, qi: (gi, qi, 0)),
                  pl.BlockSpec((nslc, ncb), lambda bi, gi, qi: (0, 0))],
        out_specs=[pl.BlockSpec((None, tq, gw), lambda bi, gi, qi: (bi, qi, gi)),
                   pl.BlockSpec((None, None, tq, LANES), lambda bi, gi, qi: (bi, gi, qi, 0))],
        out_shape=[jax.ShapeDtypeStruct((b, t, NSA_HEADS * HEAD_DIM), F32),
                   jax.ShapeDtypeStruct((b, g, t, LANES), BF16)],
        scratch_shapes=[pltpu.VMEM((nslc, tq), F32)],
        compiler_params=_cparams(3),
        name="nsa_cmp_select",
    )(p3, kvc, kvc, cbias, overlap)


def _attn_kernel(*refs, cfg):
    kind = cfg["kind"]
    c = cfg["c"]
    ns = cfg["ns"]
    has_bias = kind != "mla"
    it = iter(refs)
    far_ref = next(it) if has_bias else None
    lam_ref = next(it) if kind == "diff" else None
    q_ref, k_ref, v_ref = next(it), next(it), next(it)
    bt_ref = next(it) if has_bias else None
    sel_ref = next(it) if kind == "slc" else None
    if kind == "win":
        gl_ref, ocmp_ref, oslc_ref = next(it), next(it), next(it)
    if kind == "diff":
        g2_ref = next(it)
    o_ref = next(it)
    qs_ref = next(it) if kind != "mla" else None
    m_ref, acc_ref, s_ref = next(it), next(it), next(it)

    hg = pl.program_id(1)
    qi = pl.program_id(2)
    dv = acc_ref.shape[1] - LANES

    gr = m_ref.shape[0] // ns

    if kind in ("slc", "win"):
        for h in range(ns):
            qs_ref[h * gr:(h + 1) * gr, 0:HEAD_DIM] = q_ref[:, h * HEAD_DIM:(h + 1) * HEAD_DIM]
            if kind == "slc":
                qs_ref[h * gr:(h + 1) * gr, HEAD_DIM:] = sel_ref[...]
    elif kind == "diff":
        q = q_ref[...]
        lane = lax.broadcasted_iota(I32, q.shape, 1)
        qs_ref[0:gr, :] = jnp.where(lane < DIFF_QK_DIM, q, jnp.zeros_like(q))
        qs_ref[gr:2 * gr, :] = jnp.where(lane >= DIFF_QK_DIM, q, jnp.zeros_like(q))
    else:
        qs_ref = q_ref
    m_ref[...] = jnp.full(m_ref.shape, NEG_INF, F32)
    acc_ref[...] = jnp.zeros(acc_ref.shape, F32)

    row = lax.broadcasted_iota(I32, (gr, c), 0)
    col = lax.broadcasted_iota(I32, (gr, c), 1)

    sw = min(c, ATT_C)

    def qk(k0, w):
        k0 = pl.multiple_of(k0, c)
        kc = k_ref[pl.ds(k0, w), :]
        for h in range(ns):
            qh = qs_ref[h * gr:(h + 1) * gr, :]
            for u in range(w // sw):
                s_ref[h * gr:(h + 1) * gr, u * sw:(u + 1) * sw] = _dot_nt(qh, kc[u * sw:(u + 1) * sw])

    def consume(k0, w, mode, off):
        k0 = pl.multiple_of(k0, c)
        vc = v_ref[pl.ds(k0, w), :]
        v_ext = jnp.concatenate([vc, jnp.ones((w, LANES), vc.dtype)], axis=1)
        nsub = w // sw
        mask = None
        if mode == "diag" and kind != "mla":
            mask = row >= col
        elif mode == "edge":
            mask = col > row
        for h in range(ns):
            r0, r1 = h * gr, (h + 1) * gr
            s = s_ref[r0:r1, off:off + w]
            shift = None
            if has_bias:
                hb = hg * ns + h if kind in ("slc", "win") else hg
                if mode in ("far", "edge"):
                    shift = far_ref[hb]
                else:
                    s = s + bt_ref[h if kind in ("slc", "win") else 0, 0 if mode == "diag" else 1]
            if kind == "mla" and mode == "diag":
                mask = row + h * gr >= col
            if mask is not None:
                s = jnp.where(mask, s, NEG_INF)
            mx = jnp.max(s, axis=-1, keepdims=True)
            if shift is not None:
                mx = mx + shift
            m_prev = m_ref[r0:r1]
            m_new = jnp.maximum(m_prev, mx)
            alpha = jnp.exp(m_prev - m_new)
            sub = m_new if shift is None else m_new - shift
            p = jnp.exp(s - _rep(sub, w // LANES))
            if mask is not None:
                p = jnp.where(mask, p, 0.0)
            pb = p.astype(BF16)
            pv = _dot(pb[:, :sw], v_ext[:sw])
            for u in range(1, nsub):
                pv = pv + _dot(pb[:, u * sw:(u + 1) * sw], v_ext[u * sw:(u + 1) * sw])
            acc_ref[r0:r1] = _rep(alpha, acc_ref.shape[1] // LANES) * acc_ref[r0:r1] + pv
            m_ref[r0:r1] = m_new

    if kind == "win":
        nback = WINDOW // c
        assert nback * c == WINDOW and nback >= 2
        for back in range(nback, 0, -1):
            mode = "edge" if back == nback else ("near" if back == 1 else "far")

            @pl.when(qi - back >= 0)
            def _(back=back, mode=mode):
                qk((qi - back) * c, c)
                consume((qi - back) * c, c, mode, 0)
        qk(qi * c, c)
        consume(qi * c, c, "diag", 0)
    elif kind == "mla":
        qk(0, c)

        def far_body(j, carry):
            consume(j * c, c, "far", 0)
            qk((j + 1) * c, c)
            return carry
        lax.fori_loop(0, qi, far_body, 0)
        consume(qi * c, c, "diag", 0)
    else:
        n_pair = jnp.maximum(qi - 1, 0) // 2
        base = n_pair * (2 * c)
        qk(0, 2 * c)

        def far_body(j, carry):
            consume(j * (2 * c), 2 * c, "far", 0)
            qk((j + 1) * (2 * c), 2 * c)
            return carry
        lax.fori_loop(0, n_pair, far_body, 0)

        @pl.when(qi % 2 == 1)
        def _():
            consume(base, c, "near", 0)
            consume(base + c, c, "diag", c)

        @pl.when((qi % 2 == 0) & (qi >= 2))
        def _():
            consume(base, c, "far", 0)
            consume(base + c, c, "near", c)
            qk(base + 2 * c, 2 * c)
            consume(base + 2 * c, c, "diag", 0)

        @pl.when(qi == 0)
        def _():
            consume(0, c, "diag", 0)

    def out_stream(h):
        r0, r1 = h * gr, (h + 1) * gr
        return acc_ref[r0:r1, :dv] / jnp.maximum(acc_ref[r0:r1, dv:], 1e-30)

    if kind == "slc":
        for h in range(ns):
            o_ref[:, h * dv:(h + 1) * dv] = out_stream(h).astype(o_ref.dtype)
    elif kind == "win":
        gate = jax.nn.sigmoid(gl_ref[...])
        for h in range(ns):
            sl = slice(h * dv, (h + 1) * dv)
            o = (gate[:, 3 * h:3 * h + 1] * ocmp_ref[:, sl]
                 + gate[:, 3 * h + 1:3 * h + 2] * oslc_ref[:, sl]
                 + gate[:, 3 * h + 2:3 * h + 3] * out_stream(h))
            o_ref[:, sl] = o.astype(o_ref.dtype)
    elif kind == "diff":
        o = out_stream(0) - lam_ref[0] * out_stream(1)
        y = o * lax.rsqrt(jnp.mean(o * o, axis=-1, keepdims=True) + RMS_EPS)
        o_ref[...] = (y * g2_ref[...]).astype(o_ref.dtype)
    else:
        for h in range(ns):
            o_ref[h * gr:(h + 1) * gr, :] = out_stream(h).astype(o_ref.dtype)


def _attn_call(kind, scalars, tensors, in_specs, out_spec, out_shape, grid, *, c, ns, gr, dk, dv):
    scratch = []
    if kind != "mla":
        scratch.append(pltpu.VMEM((ns * gr, dk), BF16))
    assert dv == LANES
    sw_cols = 2 * c if kind in ("slc", "diff") else c
    scratch += [pltpu.VMEM((ns * gr, LANES), F32), pltpu.VMEM((ns * gr, dv + LANES), F32),
                pltpu.VMEM((ns * gr, sw_cols), F32)]
    assert kind not in ("slc", "diff") or (grid[2] % 2 == 0 and grid[2] >= 2)
    cfg = dict(kind=kind, c=c, ns=ns)
    return pl.pallas_call(
        functools.partial(_attn_kernel, cfg=cfg),
        grid_spec=pltpu.PrefetchScalarGridSpec(
            num_scalar_prefetch=len(scalars), grid=grid, in_specs=in_specs,
            out_specs=out_spec, scratch_shapes=scratch),
        out_shape=out_shape,
        compiler_params=_cparams(3),
        name="attn_" + kind,
    )(*scalars, *tensors)


def nsa_slc(p3, k_aug, sel, tiles, far, *, c=ATT_C):
    b, t, _ = p3.shape
    g, hpg, dh = NSA_KV_HEADS, NSA_HPG, HEAD_DIM
    vb = 14
    in_specs = [pl.BlockSpec((None, c, hpg * dh), lambda bi, gi, qi, *_: (bi, qi, gi)),
                pl.BlockSpec((None, t, dh + LANES), lambda bi, gi, qi, *_: (bi, 0, gi)),
                pl.BlockSpec((None, t, dh), lambda bi, gi, qi, *_: (bi, 0, vb + gi)),
                pl.BlockSpec((hpg, 2, c, c), lambda bi, gi, qi, *_: (gi, 0, 0, 0)),
                pl.BlockSpec((None, None, c, LANES), lambda bi, gi, qi, *_: (bi, gi, qi, 0))]
    return _attn_call(
        "slc", [far], [p3, k_aug, p3, tiles, sel], in_specs,
        pl.BlockSpec((None, c, hpg * dh), lambda bi, gi, qi, *_: (bi, qi, gi)),
        jax.ShapeDtypeStruct((b, t, NSA_HEADS * dh), F32), (b, g, t // c),
        c=c, ns=hpg, gr=c, dk=dh + LANES, dv=dh)


def nsa_win(p3, gates3, o_cmp, o_slc, tiles, far, *, c=ATT_C):
    b, t, _ = p3.shape
    g, hpg, dh = NSA_KV_HEADS, NSA_HPG, HEAD_DIM
    kb, vb = 16, 18
    ospec = pl.BlockSpec((None, c, hpg * dh), lambda bi, gi, qi, *_: (bi, qi, gi))
    in_specs = [ospec,
                pl.BlockSpec((None, t, dh), lambda bi, gi, qi, *_: (bi, 0, kb + gi)),
                pl.BlockSpec((None, t, dh), lambda bi, gi, qi, *_: (bi, 0, vb + gi)),
                pl.BlockSpec((hpg, 2, c, c), lambda bi, gi, qi, *_: (gi, 0, 0, 0)),
                pl.BlockSpec((None, c, LANES), lambda bi, gi, qi, *_: (bi, qi, gi)),
                ospec, ospec]
    return _attn_call(
        "win", [far], [p3, p3, p3, tiles, gates3, o_cmp, o_slc], in_specs, ospec,
        jax.ShapeDtypeStruct((b, t, NSA_HEADS * dh), BF16), (b, g, t // c),
        c=c, ns=hpg, gr=c, dk=dh, dv=dh)


def diff_attn(p3, tiles, far, lam, g2, *, c=ATT_C):
    b, t, _ = p3.shape
    dh = HEAD_DIM
    qb, kb, vb = 20, 28, 36
    in_specs = [pl.BlockSpec((None, c, dh), lambda bi, hi, qi, *_: (bi, qi, qb + hi)),
                pl.BlockSpec((None, t, dh), lambda bi, hi, qi, *_: (bi, 0, kb + hi)),
                pl.BlockSpec((None, t, dh), lambda bi, hi, qi, *_: (bi, 0, vb + hi)),
                pl.BlockSpec((1, 2, c, c), lambda bi, hi, qi, *_: (hi, 0, 0, 0)),
                pl.BlockSpec((1, dh), lambda bi, hi, qi, *_: (0, 0))]
    return _attn_call(
        "diff", [far, lam], [p3, p3, p3, tiles, g2], in_specs,
        pl.BlockSpec((None, c, dh), lambda bi, hi, qi, *_: (bi, qi, hi)),
        jax.ShapeDtypeStruct((b, t, DIFF_HEADS * dh), BF16), (b, DIFF_HEADS, t // c),
        c=c, ns=2, gr=c, dk=dh, dv=dh)


def mla_attn(qf, kf, kv3, *, c=MLA_C):
    b, t, _ = qf.shape
    dk = 2 * HEAD_DIM
    c = min(c, t)
    in_specs = [pl.BlockSpec((None, c, dk), lambda bi, hi, qi: (bi, qi, hi)),
                pl.BlockSpec((None, t, dk), lambda bi, hi, qi: (bi, 0, hi)),
                pl.BlockSpec((None, t, MLA_V), lambda bi, hi, qi: (bi, 0, MLA_HEADS + hi))]
    return _attn_call(
        "mla", [], [qf, kf, kv3], in_specs,
        pl.BlockSpec((None, c, MLA_V), lambda bi, hi, qi: (bi, qi, hi)),
        jax.ShapeDtypeStruct((b, t, MLA_HEADS * MLA_V), BF16), (b, MLA_HEADS, t // c),
        c=c, ns=2, gr=c // 2, dk=dk, dv=MLA_V)


def _mla_assemble_kernel(q_ref, kv_ref, c_ref, cos_ref, sin_ref, qf_ref, kf_ref):
    h16 = MLA_HEADS
    cos = cos_ref[...]
    sin = sin_ref[...]
    lane = lax.broadcasted_iota(I32, cos.shape, 1)
    first = lane < MLA_ROPE
    kpe = c_ref[:, 0:LANES] * cos + c_ref[:, LANES:2 * LANES] * sin
    for j in range(h16 // 2):
        pe = q_ref[:, (h16 + j) * LANES:(h16 + j + 1) * LANES]
        pep = q_ref[:, (h16 + h16 // 2 + j) * LANES:(h16 + h16 // 2 + j + 1) * LANES]
        qpe = pe * cos + pep * sin
        for par in range(2):
            h = 2 * j + par
            keep = first if par == 0 else jnp.logical_not(first)
            qf_ref[:, (2 * h) * LANES:(2 * h + 1) * LANES] = q_ref[:, h * LANES:(h + 1) * LANES].astype(BF16)
            qf_ref[:, (2 * h + 1) * LANES:(2 * h + 2) * LANES] = jnp.where(keep, qpe, 0.0).astype(BF16)
            kf_ref[:, (2 * h) * LANES:(2 * h + 1) * LANES] = kv_ref[:, h * LANES:(h + 1) * LANES].astype(BF16)
            kf_ref[:, (2 * h + 1) * LANES:(2 * h + 2) * LANES] = jnp.where(keep, kpe, 0.0).astype(BF16)


def mla_assemble(qraw, kvraw, craw, cos2, sin2, seq, *, tm=256):
    n = qraw.shape[0]
    tm = min(tm, seq)
    nt = seq // tm
    wq = qraw.shape[1]
    return pl.pallas_call(
        _mla_assemble_kernel,
        grid=(n // tm,),
        in_specs=[pl.BlockSpec((tm, wq), lambda i: (i, 0)),
                  pl.BlockSpec((tm, MLA_HEADS * LANES), lambda i: (i, 0)),
                  pl.BlockSpec((tm, 2 * LANES), lambda i: (i, 4)),
                  pl.BlockSpec((tm, LANES), lambda i: (i % nt, 0)),
                  pl.BlockSpec((tm, LANES), lambda i: (i % nt, 0))],
        out_specs=[pl.BlockSpec((tm, 2 * MLA_HEADS * LANES), lambda i: (i, 0)),
                   pl.BlockSpec((tm, 2 * MLA_HEADS * LANES), lambda i: (i, 0))],
        out_shape=[jax.ShapeDtypeStruct((n, 2 * MLA_HEADS * LANES), BF16)] * 2,
        compiler_params=_cparams(1),
        name="mla_assemble",
    )(qraw, kvraw, craw, cos2, sin2)


def _mem_attn_kernel(q_ref, kv_ref, o_ref):
    hd = MEM_HEAD_DIM
    voff = MEM_HEADS * hd
    for h in range(MEM_HEADS):
        sl = slice(h * hd, (h + 1) * hd)
        s = _dot_nt(q_ref[:, sl], kv_ref[:, sl])
        m = jnp.max(s, axis=-1, keepdims=True)
        p = jnp.exp(s - m)
        p = p / jnp.sum(p, axis=-1, keepdims=True)
        o_ref[:, sl] = _dot(p.astype(BF16), kv_ref[:, voff + h * hd:voff + (h + 1) * hd]).astype(o_ref.dtype)


def mem_attn(q3, kv3, *, tq=512):
    b, t, w = q3.shape
    m = kv3.shape[1]
    tq = min(tq, t)
    return pl.pallas_call(
        _mem_attn_kernel,
        grid=(b, t // tq),
        in_specs=[pl.BlockSpec((None, tq, w), lambda bi, qi: (bi, qi, 0)),
                  pl.BlockSpec((None, m, 2 * w), lambda bi, qi: (bi, 0, 0))],
        out_specs=pl.BlockSpec((None, tq, w), lambda bi, qi: (bi, qi, 0)),
        out_shape=jax.ShapeDtypeStruct((b, t, w), BF16),
        compiler_params=_cparams(2),
        name="mem_attn",
    )(q3, kv3)


def _router_kernel(h_ref, g_ref, rw_ref, rb_ref, hn_ref, ro_ref):
    x = h_ref[...]
    hn = x * lax.rsqrt(jnp.mean(x * x, axis=-1, keepdims=True) + RMS_EPS) * g_ref[...]
    hn_ref[...] = hn
    logits = jnp.dot(hn, rw_ref[...], precision=HIGHEST, preferred_element_type=F32) + rb_ref[...]
    lane = lax.broadcasted_iota(I32, logits.shape, 1)
    lg = jnp.where(lane < N_EXPERTS, logits, -jnp.inf)
    v1 = jnp.max(lg, axis=-1, keepdims=True)
    i1 = jnp.min(jnp.where(lg == v1, lane, LANES), axis=-1, keepdims=True)
    lg2 = jnp.where(lane == i1, -jnp.inf, lg)
    v2 = jnp.max(lg2, axis=-1, keepdims=True)
    i2 = jnp.min(jnp.where(lg2 == v2, lane, LANES), axis=-1, keepdims=True)
    e2 = jnp.exp(v2 - v1)
    w1 = 1.0 / (1.0 + e2)
    w2 = e2 / (1.0 + e2)
    ro_ref[...] = jnp.where(lane == 0, i1.astype(F32),
                            jnp.where(lane == 1, i2.astype(F32),
                                      jnp.where(lane == 2, w1, jnp.where(lane == 3, w2, 0.0))))


def moe_router(h, g, rw, rb, *, tm=256):
    n, d = h.shape
    rwp = jnp.zeros((d, LANES), F32).at[:, :N_EXPERTS].set(rw.astype(F32))
    rbp = jnp.zeros((1, LANES), F32).at[0, :N_EXPERTS].set(rb.astype(F32))
    return pl.pallas_call(
        _router_kernel,
        grid=(n // tm,),
        in_specs=[pl.BlockSpec((tm, d), lambda i: (i, 0)),
                  pl.BlockSpec((1, d), lambda i: (0, 0)),
                  pl.BlockSpec((d, LANES), lambda i: (0, 0)),
                  pl.BlockSpec((1, LANES), lambda i: (0, 0))],
        out_specs=[pl.BlockSpec((tm, d), lambda i: (i, 0)),
                   pl.BlockSpec((tm, LANES), lambda i: (i, 0))],
        out_shape=[jax.ShapeDtypeStruct((n, d), F32), jax.ShapeDtypeStruct((n, LANES), F32)],
        compiler_params=_cparams(1),
        name="moe_router",
    )(h, g.reshape(1, d).astype(F32), rwp, rbp)


def _row_copy(src, dst, sem):
    return pltpu.make_async_copy(src, dst, sem)


def _dispatch_kernel(d1_ref, d2_ref, x_ref, xs_in_ref, xs_ref, sem):
    del xs_in_ref
    tm = x_ref.shape[0]
    base = pl.program_id(0) * tm

    def issue(r, carry):
        src = x_ref.at[pl.ds(r, 1), :]
        _row_copy(src, xs_ref.at[pl.ds(d1_ref[base + r], 1), :], sem).start()
        _row_copy(src, xs_ref.at[pl.ds(d2_ref[base + r], 1), :], sem).start()
        return carry

    def drain(r, carry):
        src = x_ref.at[pl.ds(r, 1), :]
        _row_copy(src, xs_ref.at[pl.ds(0, 1), :], sem).wait()
        _row_copy(src, xs_ref.at[pl.ds(0, 1), :], sem).wait()
        return carry

    lax.fori_loop(0, tm, issue, 0)
    lax.fori_loop(0, tm, drain, 0)


def moe_dispatch(hn, d1, d2, rows, *, tm=256):
    n, d = hn.shape
    return pl.pallas_call(
        _dispatch_kernel,
        grid_spec=pltpu.PrefetchScalarGridSpec(
            num_scalar_prefetch=2, grid=(n // tm,),
            in_specs=[pl.BlockSpec((tm, d), lambda i, *_: (i, 0)),
                      pl.BlockSpec(memory_space=pl.ANY)],
            out_specs=pl.BlockSpec(memory_space=pl.ANY),
            scratch_shapes=[pltpu.SemaphoreType.DMA(())]),
        out_shape=jax.ShapeDtypeStruct((rows, d), hn.dtype),
        input_output_aliases={3: 0},
        compiler_params=_cparams(1),
        name="moe_dispatch",
    )(d1, d2, hn, jnp.zeros((rows, d), hn.dtype))


def _combine_kernel(d1_ref, d2_ref, ys_ref, h_ref, ro_ref, o_ref, b1_ref, b2_ref, sem):
    tm = h_ref.shape[0]
    base = pl.program_id(0) * tm

    def issue(r, carry):
        _row_copy(ys_ref.at[pl.ds(d1_ref[base + r], 1), :], b1_ref.at[pl.ds(r, 1), :], sem).start()
        _row_copy(ys_ref.at[pl.ds(d2_ref[base + r], 1), :], b2_ref.at[pl.ds(r, 1), :], sem).start()
        return carry

    def drain(r, carry):
        _row_copy(ys_ref.at[pl.ds(0, 1), :], b1_ref.at[pl.ds(r, 1), :], sem).wait()
        _row_copy(ys_ref.at[pl.ds(0, 1), :], b2_ref.at[pl.ds(r, 1), :], sem).wait()
        return carry

    lax.fori_loop(0, tm, issue, 0)
    lax.fori_loop(0, tm, drain, 0)
    ro = ro_ref[...]
    o_ref[...] = h_ref[...] + ro[:, 2:3] * b1_ref[...] + ro[:, 3:4] * b2_ref[...]


def moe_combine(ys, h, ro, d1, d2, *, tm=256):
    n, d = h.shape
    return pl.pallas_call(
        _combine_kernel,
        grid_spec=pltpu.PrefetchScalarGridSpec(
            num_scalar_prefetch=2, grid=(n // tm,),
            in_specs=[pl.BlockSpec(memory_space=pl.ANY),
                      pl.BlockSpec((tm, d), lambda i, *_: (i, 0)),
                      pl.BlockSpec((tm, LANES), lambda i, *_: (i, 0))],
            out_specs=pl.BlockSpec((tm, d), lambda i, *_: (i, 0)),
            scratch_shapes=[pltpu.VMEM((tm, d), F32), pltpu.VMEM((tm, d), F32),
                            pltpu.SemaphoreType.DMA(())]),
        out_shape=jax.ShapeDtypeStruct((n, d), F32),
        compiler_params=_cparams(1),
        name="moe_combine",
    )(d1, d2, ys, h, ro)


def moe_block(h, g, rw, rb, w_gu, w_down, layer, *, tm_e=EXPERT_TM):
    n, d = h.shape
    tm_e = min(tm_e, n)
    hn, ro = moe_router(h, g, rw, rb)
    e1 = ro[:, 0].astype(I32)
    e2 = ro[:, 1].astype(I32)
    ind = (jax.nn.one_hot(e1, N_EXPERTS, dtype=I32) + jax.nn.one_hot(e2, N_EXPERTS, dtype=I32))
    csum = jnp.cumsum(ind, axis=0)
    rank = csum - ind
    cnt = csum[-1]
    pcnt = (cnt + tm_e - 1) // tm_e * tm_e
    ends = jnp.cumsum(pcnt)
    off = ends - pcnt
    d1 = (off[e1] + jnp.take_along_axis(rank, e1[:, None], axis=1)[:, 0]).astype(I32)
    d2 = (off[e2] + jnp.take_along_axis(rank, e2[:, None], axis=1)[:, 0]).astype(I32)
    rows = 2 * n + N_EXPERTS * tm_e
    tile_start = jnp.arange(rows // tm_e, dtype=I32) * tm_e
    tile_e = jnp.minimum(jnp.sum((tile_start[:, None] >= ends[None, :]).astype(I32), axis=1),
                         N_EXPERTS - 1).astype(I32)
    xs = moe_dispatch(hn, d1, d2, rows)
    tile_w = tile_e + layer * N_EXPERTS
    n_valid = (ends[-1] // tm_e).astype(I32).reshape(1)
    act = gmm(xs.astype(BF16), w_gu, tile_w, n_valid, tm=tm_e, out_dtype=BF16, swiglu=True)
    ys = gmm(act, w_down, tile_w, n_valid, tm=tm_e, out_dtype=F32)
    return moe_combine(ys, h, ro, d1, d2)


def _mem_block(h, mem2, b, t, i, g_ca, g_mem, wq, wkv, wo):
    n, d = h.shape
    hn = rmsnorm(h, g_ca[i], BF16)
    memn = rmsnorm(mem2, g_mem[i], BF16)
    q = mm(hn, (wq[i] * MEM_HEAD_DIM ** -0.5).astype(BF16), out_dtype=BF16)
    kv = mm(memn, wkv, group=i, out_dtype=BF16)
    o = mem_attn(q.reshape(b, t, -1), kv.reshape(b, mem2.shape[0] // b, -1))
    return mm(o.reshape(n, -1), wo, group=i, out_dtype=F32, res=h)


def _even_layer(h, b, t, li, lam_init, p, tabs):
    n, d = h.shape
    w_in = p["even_w_in"][li]
    nq, nkv = NSA_HEADS * HEAD_DIM, NSA_KV_HEADS * HEAD_DIM
    g0 = nq + 6 * nkv
    dq0 = g0 + 3 * NSA_HEADS
    dqw = DIFF_HEADS * 2 * DIFF_QK_DIM
    w_main = jnp.concatenate([w_in[:, :nq] * HEAD_DIM ** -0.5, w_in[:, nq:g0],
                              w_in[:, dq0:dq0 + dqw] * DIFF_QK_DIM ** -0.5, w_in[:, dq0 + dqw:]],
                             axis=1).astype(BF16)
    w_gate = jnp.zeros((d, NSA_KV_HEADS * LANES), F32)
    for g in range(NSA_KV_HEADS):
        w_gate = w_gate.at[:, g * LANES:g * LANES + 3 * NSA_HPG].set(
            w_in[:, g0 + g * 3 * NSA_HPG:g0 + (g + 1) * 3 * NSA_HPG])
    xn = rmsnorm(h, p["norm_mix"][2 * li], BF16)
    proj = mm(xn, w_main, out_dtype=BF16)
    gates = mm(xn, w_gate.astype(BF16), out_dtype=F32)
    p3 = proj.reshape(b, t, -1)

    kvs = p3[:, :, nq:nq + 2 * nkv].reshape(b, t, 2, NSA_KV_HEADS, HEAD_DIM)
    acat = kvs.transpose(2, 0, 3, 1, 4).reshape(2, b, NSA_KV_HEADS, t // CMP_STRIDE, CMP_STRIDE * HEAD_DIM)
    w1 = p["nsa_cmp_w1"][li]
    half = CMP_STRIDE * HEAD_DIM
    w1cat = jnp.concatenate([w1[:, :half], w1[:, half:]], axis=2).astype(BF16)
    pos = p["nsa_cmp_pos"][li].reshape(2, 2, half)
    pos2 = jnp.zeros((2, 8, half), F32).at[:, :2].set(pos).astype(BF16)
    kvc = nsa_compress(acat, w1cat, p["nsa_cmp_w2"][li].astype(BF16), pos2)
    o_cmp, sel = nsa_cmp_select(p3, kvc, tabs["cbias"], tabs["overlap"])
    k_slc = p3[:, :, nq + 2 * nkv:nq + 3 * nkv].reshape(b, t, NSA_KV_HEADS, HEAD_DIM)
    k_aug = jnp.concatenate([k_slc, jnp.broadcast_to(tabs["blk_onehot"][None, :, None, :],
                                                     (b, t, NSA_KV_HEADS, LANES))], axis=-1)
    o_slc = nsa_slc(p3, k_aug.reshape(b, t, -1), sel, tabs["tiles_a"], tabs["far_a"])
    o_a = nsa_win(p3, gates.reshape(b, t, -1), o_cmp, o_slc, tabs["tiles_a"], tabs["far_a"])

    lq1, lk1, lq2, lk2 = p["diff_lambda"][li].astype(F32)
    lam = jnp.exp(jnp.sum(lq1 * lk1)) - jnp.exp(jnp.sum(lq2 * lk2)) + lam_init
    g2 = (p["diff_subln"][li].astype(F32) * (1.0 - lam_init)).reshape(1, HEAD_DIM)
    o_b = diff_attn(p3, tabs["tiles_b"], tabs["far_b"], lam.reshape(1), g2)

    mix_in = jnp.concatenate([o_a, o_b], axis=-1).reshape(n, -1)
    h = mm(mix_in, p["even_w_out"], group=li, out_dtype=F32, res=h)
    return h


def _odd_layer(h, b, t, li, p, tabs):
    n, d = h.shape
    hh, nope, rp = MLA_HEADS, MLA_NOPE, MLA_ROPE
    scale = (nope + rp) ** -0.5

    def rot(w):
        w = w.reshape(w.shape[0], -1, 2, rp // 2)
        return jnp.stack([-w[:, :, 1], w[:, :, 0]], axis=2).reshape(w.shape[0], -1)

    w_in = p["odd_w_in"][li]
    kpe_w = w_in[:, MLA_Q_RANK + MLA_KV_RANK:]
    w_in_ext = jnp.concatenate([w_in[:, :MLA_Q_RANK + MLA_KV_RANK], kpe_w, kpe_w, rot(kpe_w), rot(kpe_w)],
                               axis=1).astype(BF16)
    wq = (p["mla_w_q_up"][li] * scale).reshape(MLA_Q_RANK, hh, nope + rp)
    wq_pe = wq[:, :, nope:].reshape(MLA_Q_RANK, hh * rp)
    wq_ext = jnp.concatenate([wq[:, :, :nope].reshape(MLA_Q_RANK, hh * nope), wq_pe, rot(wq_pe)],
                             axis=1).astype(BF16)
    wkv = p["mla_w_kv_up"][li].reshape(MLA_KV_RANK, hh, nope + MLA_V)
    wkv_ext = jnp.concatenate([wkv[:, :, :nope].reshape(MLA_KV_RANK, -1),
                               wkv[:, :, nope:].reshape(MLA_KV_RANK, -1)], axis=1).astype(BF16)

    xn = rmsnorm(h, p["norm_mix"][2 * li + 1], BF16)
    craw = mm(xn, w_in_ext, out_dtype=F32)
    cqn = rmsnorm(craw, p["mla_q_norm"][li], BF16, width=MLA_Q_RANK, col_block=0)
    ckvn = rmsnorm(craw, p["mla_kv_norm"][li], BF16, width=MLA_KV_RANK, col_block=1)
    qraw = mm(cqn, wq_ext, out_dtype=F32)
    kvraw = mm(ckvn, wkv_ext, out_dtype=BF16)
    qf, kf = mla_assemble(qraw, kvraw, craw, tabs["cos2"], tabs["sin2"], t)
    o = mla_attn(qf.reshape(b, t, -1), kf.reshape(b, t, -1), kvraw.reshape(b, t, -1))
    return mm(o.reshape(n, -1), p["odd_w_out"], group=li, out_dtype=F32, res=h)


def _tables(rel_bias, t):
    bias_a, bias_b = rel_bias[:, :NSA_HEADS], rel_bias[:, NSA_HEADS:]
    tiles_a, far_a = _bias_tiles(bias_a, ATT_C)
    tiles_b, far_b = _bias_tiles(bias_b, ATT_C)
    ncb = t // CMP_STRIDE
    nslc = t // SLC_BLOCK
    tt = jnp.arange(t)
    blk_start = jnp.arange(ncb) * CMP_STRIDE
    cbias = _bias_lookup(tt[:, None] - (blk_start + CMP_BLOCK - 1)[None, :], bias_a)
    jj = jnp.arange(nslc)
    n_cmp = (t - CMP_BLOCK) // CMP_STRIDE + 1
    overlap = ((blk_start[:, None] < (jj[None, :] + 1) * SLC_BLOCK)
               & (blk_start[:, None] + CMP_BLOCK > jj[None, :] * SLC_BLOCK)
               & (jnp.arange(ncb)[:, None] < n_cmp)).astype(F32)
    blk_onehot = (jnp.arange(LANES)[None, :] == (tt[:, None] // SLC_BLOCK)).astype(BF16)
    inv_freq = ROPE_THETA ** (-jnp.arange(0, MLA_ROPE, 2, dtype=F32) / MLA_ROPE)
    ang = jnp.arange(t, dtype=F32)[:, None] * inv_freq
    cos2 = jnp.tile(jnp.cos(ang), (1, 2 * LANES // MLA_ROPE))
    sin2 = jnp.tile(jnp.sin(ang), (1, 2 * LANES // MLA_ROPE))
    return dict(tiles_a=tiles_a, far_a=far_a, tiles_b=tiles_b, far_b=far_b, cbias=cbias,
                overlap=overlap.T, blk_onehot=blk_onehot, cos2=cos2, sin2=sin2)


def kernel(x, mem, rel_bias, norm_mix, norm_mem, norm_ca, norm_ffn, norm_final, even_w_in, even_w_out, nsa_cmp_pos, nsa_cmp_w1, nsa_cmp_w2, diff_lambda, diff_subln, ffn_w_gu, ffn_w_down, odd_w_in, mla_q_norm, mla_kv_norm, mla_w_q_up, mla_w_kv_up, odd_w_out, router_w, router_b, moe_w_gu, moe_w_down, ca_wq, ca_wkv, ca_wo):
    p = dict(norm_mix=norm_mix, even_w_in=even_w_in, even_w_out=even_w_out, nsa_cmp_pos=nsa_cmp_pos,
             nsa_cmp_w1=nsa_cmp_w1, nsa_cmp_w2=nsa_cmp_w2, diff_lambda=diff_lambda, diff_subln=diff_subln,
             odd_w_in=odd_w_in, mla_q_norm=mla_q_norm, mla_kv_norm=mla_kv_norm, mla_w_q_up=mla_w_q_up,
             mla_w_kv_up=mla_w_kv_up, odd_w_out=odd_w_out)
    b, t, d = x.shape
    depth = norm_mix.shape[0]
    tabs = _tables(rel_bias, t)
    h = x.reshape(b * t, d).astype(F32)
    mem2 = mem.reshape(-1, d).astype(F32)
    for i in range(depth):
        li = i // 2
        if i % 2 == 0:
            h = _even_layer(h, b, t, li, 0.8 - 0.6 * math.exp(-0.3 * i), p, tabs)
        else:
            h = _odd_layer(h, b, t, li, p, tabs)
        h = _mem_block(h, mem2, b, t, i, norm_ca, norm_mem, ca_wq, ca_wkv, ca_wo)
        if i % 2 == 0:
            hn = rmsnorm(h, norm_ffn[i], BF16)
            act = mm(hn, ffn_w_gu, group=li, out_dtype=BF16, swiglu=True)
            h = mm(act, ffn_w_down, group=li, out_dtype=F32, res=h)
        else:
            h = moe_block(h, norm_ffn[i], router_w[li], router_b[li],
                          moe_w_gu.reshape((-1,) + moe_w_gu.shape[2:]),
                          moe_w_down.reshape((-1,) + moe_w_down.shape[2:]), li)
    return rmsnorm(h, norm_final, x.dtype).reshape(b, t, d)
```

```python
import functools
import math

import jax
import jax.numpy as jnp
from jax import lax
from jax.experimental import pallas as pl
from jax.experimental.pallas import tpu as pltpu

F32 = jnp.float32
BF16 = jnp.bfloat16
I32 = jnp.int32

HEAD_DIM = 128
NSA_HEADS = 8
NSA_KV_HEADS = 2
NSA_HPG = NSA_HEADS // NSA_KV_HEADS
CMP_BLOCK = 32
CMP_STRIDE = 16
SLC_BLOCK = 64
SLC_TOPK = 16
WINDOW = 512
DIFF_HEADS = 8
DIFF_QK_DIM = 64
MLA_HEADS = 16
MLA_Q_RANK = 512
MLA_KV_RANK = 512
MLA_NOPE = 128
MLA_ROPE = 64
MLA_V = 128
ROPE_THETA = 10000.0
MEM_HEADS = 4
MEM_HEAD_DIM = 128
N_EXPERTS = 8
NUM_BUCKETS = 32
MAX_DISTANCE = 128
RMS_EPS = 1e-6
NEG_INF = -1e30
BIG = 1e30

LANES = 128
MXU_COLS = 256
GMM_VMEM_BUDGET = 46 * 1024 * 1024
VMEM_LIMIT = 56 * 1024 * 1024
ATT_C = 256
BIG_C = 512
MLA_C = 512
EXPERT_TM = 512
HIGHEST = lax.Precision.HIGHEST


def _cparams(n_axes):
    return pltpu.CompilerParams(dimension_semantics=("arbitrary",) * n_axes,
                                vmem_limit_bytes=VMEM_LIMIT)


def _dot(a, b):
    return jnp.dot(a, b, preferred_element_type=F32)


def _rep(x, n):
    return x if n == 1 else jnp.concatenate([x] * n, axis=1)


def _rep_cat(xs):
    return xs[0] if len(xs) == 1 else jnp.concatenate(xs, axis=1)


def _dot_nt(a, b):
    return lax.dot_general(a, b, (((1,), (1,)), ((), ())), preferred_element_type=F32)


def _rms_kernel(x_ref, g_ref, o_ref):
    x = x_ref[...].astype(F32)
    y = x * lax.rsqrt(jnp.mean(x * x, axis=-1, keepdims=True) + RMS_EPS)
    o_ref[...] = (y * g_ref[...]).astype(o_ref.dtype)


def rmsnorm(x, g, out_dtype, width=None, col_block=0, tm=512):
    m = x.shape[0]
    width = width or x.shape[1]
    tm = min(tm, m)
    return pl.pallas_call(
        _rms_kernel,
        grid=(m // tm,),
        in_specs=[pl.BlockSpec((tm, width), lambda i: (i, col_block)),
                  pl.BlockSpec((1, width), lambda i: (0, 0))],
        out_specs=pl.BlockSpec((tm, width), lambda i: (i, 0)),
        out_shape=jax.ShapeDtypeStruct((m, width), out_dtype),
        compiler_params=_cparams(1),
        name="rmsnorm",
    )(x, g.reshape(1, width).astype(F32))


def _gmm_kernel(te_ref, nv_ref, x_ref, w_ref, *rest, swiglu, has_res, cast_w):
    nw = 2 if swiglu else 1
    w_refs = (w_ref,) + tuple(rest[:nw - 1])
    res_ref = rest[nw - 1] if has_res else None
    o_ref = rest[nw - 1 + int(has_res)]
    i = pl.program_id(1)
    if cast_w:
        wb_refs = rest[nw + int(has_res):]
        prev = te_ref[jnp.maximum(i - 1, 0)]

        @pl.when((i == 0) | (te_ref[i] != prev))
        def _():
            for src, dst in zip(w_refs, wb_refs):
                dst[...] = src[...].astype(BF16)
        w_refs = wb_refs

    @pl.when(i < nv_ref[0])
    def _():
        x = x_ref[...].astype(BF16)
        acc = _dot(x, w_refs[0][...])
        if swiglu:
            acc = acc * jax.nn.sigmoid(acc) * _dot(x, w_refs[1][...])
        if has_res:
            acc = acc + res_ref[...]
        o_ref[...] = acc.astype(o_ref.dtype)

    @pl.when(i >= nv_ref[0])
    def _():
        o_ref[...] = jnp.zeros(o_ref.shape, o_ref.dtype)


def _gmm_tn(k, n, tm, x_bytes, w_bytes, out_bytes, n_w, has_res):
    best = None
    for tn in range(LANES, n + 1, LANES):
        if n % tn or (tn % MXU_COLS and tn != n):
            continue
        fp = (n_w * k * tn * w_bytes * 2 + (n_w * k * tn * 2 if w_bytes == 4 else 0)
              + tm * k * x_bytes * 2 + tm * tn * out_bytes * 2 + tm * tn * 4 * n_w
              + (tm * tn * 4 * 2 if has_res else 0))
        if fp <= GMM_VMEM_BUDGET:
            best = tn
    assert best is not None, (k, n, tm)
    return best


def gmm(x, w, tile_e, n_valid, *, tm, out_dtype, swiglu=False, res=None):
    r, k = x.shape
    n = w.shape[2] // 2 if swiglu else w.shape[2]
    n_w = 2 if swiglu else 1
    cast_w = w.dtype != BF16
    tn = _gmm_tn(k, n, tm, x.dtype.itemsize, w.dtype.itemsize, jnp.dtype(out_dtype).itemsize, n_w,
                 res is not None)
    nj = n // tn
    in_specs = [pl.BlockSpec((tm, k), lambda j, i, te, nv: (jnp.minimum(i, nv[0] - 1), 0)),
                pl.BlockSpec((None, k, tn), lambda j, i, te, nv: (te[i], 0, j))]
    args = [x, w]
    if swiglu:
        in_specs.append(pl.BlockSpec((None, k, tn), lambda j, i, te, nv: (te[i], 0, j + nj)))
        args.append(w)
    if res is not None:
        in_specs.append(pl.BlockSpec((tm, tn), lambda j, i, te, nv: (i, j)))
        args.append(res)
    scratch = [pltpu.VMEM((k, tn), BF16)] * (n_w if cast_w else 0)
    return pl.pallas_call(
        functools.partial(_gmm_kernel, swiglu=swiglu, has_res=res is not None, cast_w=cast_w),
        grid_spec=pltpu.PrefetchScalarGridSpec(
            num_scalar_prefetch=2,
            grid=(nj, r // tm),
            in_specs=in_specs,
            out_specs=pl.BlockSpec((tm, tn), lambda j, i, te, nv: (i, j)),
            scratch_shapes=scratch),
        out_shape=jax.ShapeDtypeStruct((r, n), out_dtype),
        compiler_params=_cparams(2),
        name="gmm_swiglu" if swiglu else "gmm",
    )(tile_e, n_valid, *args)


def mm(x, w, *, out_dtype, tm=512, swiglu=False, res=None, group=0):
    m = x.shape[0]
    tm = min(tm, m)
    if w.ndim == 2:
        w = w[None]
    return gmm(x, w, jnp.full((m // tm,), group, I32), jnp.full((1,), m // tm, I32), tm=tm,
               out_dtype=out_dtype, swiglu=swiglu, res=res)


def _t5_bucket(dist):
    n = jnp.maximum(dist, 0)
    max_exact = NUM_BUCKETS // 2
    nf = jnp.maximum(n, 1).astype(F32)
    large = max_exact + (jnp.log(nf / max_exact) / math.log(MAX_DISTANCE / max_exact)
                         * (NUM_BUCKETS - max_exact)).astype(I32)
    large = jnp.minimum(large, NUM_BUCKETS - 1)
    return jnp.where(n < max_exact, n, large)


def _bias_lookup(dist, tbl):
    oh = jax.nn.one_hot(_t5_bucket(dist), NUM_BUCKETS, dtype=F32)
    return jnp.einsum("...k,kh->h...", oh, tbl.astype(F32), precision=HIGHEST)


def _bias_tiles(tbl, c):
    i = jnp.arange(c)[:, None]
    j = jnp.arange(c)[None, :]
    tiles = jnp.stack([_bias_lookup(i - j, tbl), _bias_lookup(c + i - j, tbl)], axis=1)
    far = _bias_lookup(jnp.asarray(2 * c), tbl)
    assert c + 1 >= MAX_DISTANCE
    return tiles, far


def _compress_kernel(a_ref, w1_ref, w2_ref, pos_ref, o_ref):
    nblk = a_ref.shape[0]
    pq = _dot(a_ref[...], w1_ref[...])
    r = _dot(pos_ref[...], w1_ref[...])
    c = r[0:1, :HEAD_DIM] + r[1:2, HEAD_DIM:]
    hp = pq[:, :HEAD_DIM] + pltpu.roll(pq[:, HEAD_DIM:], nblk - 1, 0) + c
    hdn = jax.nn.gelu(hp)
    o_ref[...] = _dot(hdn.astype(BF16), w2_ref[...]).astype(o_ref.dtype)


def nsa_compress(acat, w1cat, w2, pos2):
    _, b, g, nblk, wid = acat.shape
    return pl.pallas_call(
        _compress_kernel,
        grid=(2, b, g),
        in_specs=[pl.BlockSpec((None, None, None, nblk, wid), lambda s, bi, gi: (s, bi, gi, 0, 0)),
                  pl.BlockSpec((None, wid, 2 * HEAD_DIM), lambda s, bi, gi: (s, 0, 0)),
                  pl.BlockSpec((None, HEAD_DIM, HEAD_DIM), lambda s, bi, gi: (s, 0, 0)),
                  pl.BlockSpec((None, 8, wid), lambda s, bi, gi: (s, 0, 0))],
        out_specs=pl.BlockSpec((None, None, None, nblk, HEAD_DIM), lambda s, bi, gi: (s, bi, gi, 0, 0)),
        out_shape=jax.ShapeDtypeStruct((2, b, g, nblk, HEAD_DIM), BF16),
        compiler_params=_cparams(3),
        name="nsa_compress",
    )(acat, w1cat, w2, pos2)


def _nsa_cmp_kernel(q_ref, kc_ref, vc_ref, cb_ref, ov_ref, o_ref, sel_ref, sc_ref, *, tq, topk):
    qi = pl.program_id(2)
    ncb = kc_ref.shape[0]
    nslc = ov_ref.shape[0]
    t = qi * tq + lax.broadcasted_iota(I32, (tq, ncb), 0)
    n = lax.broadcasted_iota(I32, (tq, ncb), 1)
    mask = n * CMP_STRIDE + (CMP_BLOCK - 1) <= t
    kc = kc_ref[...]
    vc = vc_ref[...]
    psum = jnp.zeros((tq, ncb), F32)
    for h in range(NSA_HPG):
        qh = q_ref[:, h * HEAD_DIM:(h + 1) * HEAD_DIM]
        s = jnp.where(mask, _dot_nt(qh, kc) + cb_ref[h], NEG_INF)
        m = jnp.max(s, axis=-1, keepdims=True)
        p = jnp.where(mask, jnp.exp(s - m), 0.0)
        p = p / jnp.maximum(jnp.sum(p, axis=-1, keepdims=True), 1e-30)
        o_ref[:, h * HEAD_DIM:(h + 1) * HEAD_DIM] = _dot(p.astype(BF16), vc).astype(o_ref.dtype)
        psum = psum + p
    imp = lax.dot_general(ov_ref[...], psum, (((1,), (1,)), ((), ())), precision=HIGHEST,
                          preferred_element_type=F32)
    tt = qi * tq + lax.broadcasted_iota(I32, (nslc, tq), 1)
    j = lax.broadcasted_iota(I32, (nslc, tq), 0)
    cur = tt // SLC_BLOCK
    forced = (j == 0) | (j == cur) | (j == cur - 1)
    visible = j * SLC_BLOCK <= tt
    sc_ref[...] = jnp.where(forced, BIG, jnp.where(visible, imp, NEG_INF))
    score = sc_ref[...]
    rank = jnp.zeros((nslc, tq), F32)
    for i in range(nslc):
        other = sc_ref[i:i + 1, :]
        tie = jnp.where(j > i, 1.0, 0.0)
        rank = rank + jnp.where(other > score, 1.0, jnp.where(other == score, tie, 0.0))
    neg = jnp.where(rank < topk, 0.0, NEG_INF)
    neg = jnp.concatenate([neg, jnp.zeros((LANES - nslc, tq), F32)], axis=0)
    sel_ref[...] = neg.T.astype(sel_ref.dtype)


def nsa_cmp_select(p3, kvc, cbias, overlap, *, tq=ATT_C):
    b, t, _ = p3.shape
    g = NSA_KV_HEADS
    ncb = kvc.shape[3]
    nslc = t // SLC_BLOCK
    assert nslc <= LANES and nslc % 8 == 0
    gw = NSA_HPG * HEAD_DIM
    return pl.pallas_call(
        functools.partial(_nsa_cmp_kernel, tq=tq, topk=min(SLC_TOPK, nslc)),
        grid=(b, g, t // tq),
        in_specs=[pl.BlockSpec((None, tq, gw), lambda bi, gi, qi: (bi, qi, gi)),
                  pl.BlockSpec((None, None, None, ncb, HEAD_DIM), lambda bi, gi, qi: (0, bi, gi, 0, 0)),
                  pl.BlockSpec((None, None, None, ncb, HEAD_DIM), lambda bi, gi, qi: (1, bi, gi, 0, 0)),
                  pl.BlockSpec((NSA_HPG, tq, ncb), lambda bi, gi, qi: (gi, qi, 0)),
                  pl.BlockSpec((nslc, ncb), lambda bi, gi, qi: (0, 0))],
        out_specs=[pl.BlockSpec((None, tq, gw), lambda bi, gi, qi: (bi, qi, gi)),
                   pl.BlockSpec((None, None, tq, LANES), lambda bi, gi, qi: (bi, gi, qi, 0))],
        out_shape=[jax.ShapeDtypeStruct((b, t, NSA_HEADS * HEAD_DIM), F32),
                   jax.ShapeDtypeStruct((b, g, t, LANES), BF16)],
        scratch_shapes=[pltpu.VMEM((nslc, tq), F32)],
        compiler_params=_cparams(3),
        name="nsa_cmp_select",
    )(p3, kvc, kvc, cbias, overlap)


def _attn_kernel(*refs, cfg):
    kind = cfg["kind"]
    c = cfg["c"]
    ns = cfg["ns"]
    has_bias = kind != "mla"
    it = iter(refs)
    far_ref = next(it) if has_bias else None
    lam_ref = next(it) if kind == "diff" else None
    q_ref, k_ref, v_ref = next(it), next(it), next(it)
    bt_ref = next(it) if has_bias else None
    sel_ref = next(it) if kind == "slc" else None
    if kind == "win":
        gl_ref, ocmp_ref, oslc_ref = next(it), next(it), next(it)
    if kind == "diff":
        g2_ref = next(it)
    o_ref = next(it)
    qs_ref = next(it) if kind != "mla" else None
    m_ref, acc_ref, s_ref = next(it), next(it), next(it)

    hg = pl.program_id(1)
    qi = pl.program_id(2)
    dv = acc_ref.shape[1] - LANES

    gr = m_ref.shape[0] // ns

    if kind in ("slc", "win"):
        for h in range(ns):
            qs_ref[h * gr:(h + 1) * gr, 0:HEAD_DIM] = q_ref[:, h * HEAD_DIM:(h + 1) * HEAD_DIM]
            if kind == "slc":
                qs_ref[h * gr:(h + 1) * gr, HEAD_DIM:] = sel_ref[...]
    elif kind == "diff":
        q = q_ref[...]
        lane = lax.broadcasted_iota(I32, q.shape, 1)
        qs_ref[0:gr, :] = jnp.where(lane < DIFF_QK_DIM, q, jnp.zeros_like(q))
        qs_ref[gr:2 * gr, :] = jnp.where(lane >= DIFF_QK_DIM, q, jnp.zeros_like(q))
    else:
        qs_ref = q_ref
    m_ref[...] = jnp.full(m_ref.shape, NEG_INF, F32)
    acc_ref[...] = jnp.zeros(acc_ref.shape, F32)

    row = lax.broadcasted_iota(I32, (gr, c), 0)
    col = lax.broadcasted_iota(I32, (gr, c), 1)

    sw = min(c, ATT_C)

    def qk(k0, w):
        k0 = pl.multiple_of(k0, c)
        kc = k_ref[pl.ds(k0, w), :]
        for h in range(ns):
            qh = qs_ref[h * gr:(h + 1) * gr, :]
            for u in range(w // sw):
                s_ref[h * gr:(h + 1) * gr, u * sw:(u + 1) * sw] = _dot_nt(qh, kc[u * sw:(u + 1) * sw])

    def consume(k0, w, mode, off):
        k0 = pl.multiple_of(k0, c)
        vc = v_ref[pl.ds(k0, w), :]
        v_ext = jnp.concatenate([vc, jnp.ones((w, LANES), vc.dtype)], axis=1)
        nsub = w // sw
        mask = None
        if mode == "diag" and kind != "mla":
            mask = row >= col
        elif mode == "edge":
            mask = col > row
        for h in range(ns):
            r0, r1 = h * gr, (h + 1) * gr
            s = s_ref[r0:r1, off:off + w]
            shift = None
            if has_bias:
                hb = hg * ns + h if kind in ("slc", "win") else hg
                if mode in ("far", "edge"):
                    shift = far_ref[hb]
                else:
                    s = s + bt_ref[h if kind in ("slc", "win") else 0, 0 if mode == "diag" else 1]
            if kind == "mla" and mode == "diag":
                mask = row + h * gr >= col
            if mask is not None:
                s = jnp.where(mask, s, NEG_INF)
            mx = jnp.max(s, axis=-1, keepdims=True)
            if shift is not None:
                mx = mx + shift
            m_prev = m_ref[r0:r1]
            m_new = jnp.maximum(m_prev, mx)
            alpha = jnp.exp(m_prev - m_new)
            sub = m_new if shift is None else m_new - shift
            p = jnp.exp(s - _rep(sub, w // LANES))
            if mask is not None:
                p = jnp.where(mask, p, 0.0)
            pb = p.astype(BF16)
            pv = _dot(pb[:, :sw], v_ext[:sw])
            for u in range(1, nsub):
                pv = pv + _dot(pb[:, u * sw:(u + 1) * sw], v_ext[u * sw:(u + 1) * sw])
            acc_ref[r0:r1] = _rep(alpha, acc_ref.shape[1] // LANES) * acc_ref[r0:r1] + pv
            m_ref[r0:r1] = m_new

    if kind == "win":
        nback = WINDOW // c
        assert nback * c == WINDOW and nback >= 2
        for back in range(nback, 0, -1):
            mode = "edge" if back == nback else ("near" if back == 1 else "far")

            @pl.when(qi - back >= 0)
            def _(back=back, mode=mode):
                qk((qi - back) * c, c)
                consume((qi - back) * c, c, mode, 0)
        qk(qi * c, c)
        consume(qi * c, c, "diag", 0)
    else:
        n_pair = jnp.maximum(qi - 1, 0) // 2
        base = n_pair * (2 * c)
        qk(0, 2 * c)

        def far_body(j, carry):
            consume(j * (2 * c), 2 * c, "far", 0)
            qk((j + 1) * (2 * c), 2 * c)
            return carry
        lax.fori_loop(0, n_pair, far_body, 0)

        @pl.when(qi % 2 == 1)
        def _():
            consume(base, c, "near", 0)
            consume(base + c, c, "diag", c)

        @pl.when((qi % 2 == 0) & (qi >= 2))
        def _():
            consume(base, c, "far", 0)
            consume(base + c, c, "near", c)
            qk(base + 2 * c, 2 * c)
            consume(base + 2 * c, c, "diag", 0)

        @pl.when(qi == 0)
        def _():
            consume(0, c, "diag", 0)

    def out_stream(h):
        r0, r1 = h * gr, (h + 1) * gr
        return acc_ref[r0:r1, :dv] / jnp.maximum(acc_ref[r0:r1, dv:], 1e-30)

    if kind == "slc":
        for h in range(ns):
            o_ref[:, h * dv:(h + 1) * dv] = out_stream(h).astype(o_ref.dtype)
    elif kind == "win":
        gate = jax.nn.sigmoid(gl_ref[...])
        for h in range(ns):
            sl = slice(h * dv, (h + 1) * dv)
            o = (gate[:, 3 * h:3 * h + 1] * ocmp_ref[:, sl]
                 + gate[:, 3 * h + 1:3 * h + 2] * oslc_ref[:, sl]
                 + gate[:, 3 * h + 2:3 * h + 3] * out_stream(h))
            o_ref[:, sl] = o.astype(o_ref.dtype)
    elif kind == "diff":
        o = out_stream(0) - lam_ref[0] * out_stream(1)
        y = o * lax.rsqrt(jnp.mean(o * o, axis=-1, keepdims=True) + RMS_EPS)
        o_ref[...] = (y * g2_ref[...]).astype(o_ref.dtype)
    else:
        for h in range(ns):
            o_ref[h * gr:(h + 1) * gr, :] = out_stream(h).astype(o_ref.dtype)


def _attn_call(kind, scalars, tensors, in_specs, out_spec, out_shape, grid, *, c, ns, gr, dk, dv):
    scratch = []
    if kind != "mla":
        scratch.append(pltpu.VMEM((ns * gr, dk), BF16))
    assert dv == LANES
    sw_cols = c if kind == "win" else 2 * c
    scratch += [pltpu.VMEM((ns * gr, LANES), F32), pltpu.VMEM((ns * gr, dv + LANES), F32),
                pltpu.VMEM((ns * gr, sw_cols), F32)]
    assert kind == "win" or (grid[2] % 2 == 0 and grid[2] >= 2)
    cfg = dict(kind=kind, c=c, ns=ns)
    return pl.pallas_call(
        functools.partial(_attn_kernel, cfg=cfg),
        grid_spec=pltpu.PrefetchScalarGridSpec(
            num_scalar_prefetch=len(scalars), grid=grid, in_specs=in_specs,
            out_specs=out_spec, scratch_shapes=scratch),
        out_shape=out_shape,
        compiler_params=_cparams(3),
        name="attn_" + kind,
    )(*scalars, *tensors)


def nsa_slc(p3, k_aug, sel, tiles, far, *, c=BIG_C):
    b, t, _ = p3.shape
    c = min(c, t // 2)
    g, hpg, dh = NSA_KV_HEADS, NSA_HPG, HEAD_DIM
    vb = 14
    in_specs = [pl.BlockSpec((None, c, hpg * dh), lambda bi, gi, qi, *_: (bi, qi, gi)),
                pl.BlockSpec((None, t, dh + LANES), lambda bi, gi, qi, *_: (bi, 0, gi)),
                pl.BlockSpec((None, t, dh), lambda bi, gi, qi, *_: (bi, 0, vb + gi)),
                pl.BlockSpec((hpg, 2, c, c), lambda bi, gi, qi, *_: (gi, 0, 0, 0)),
                pl.BlockSpec((None, None, c, LANES), lambda bi, gi, qi, *_: (bi, gi, qi, 0))]
    return _attn_call(
        "slc", [far], [p3, k_aug, p3, tiles, sel], in_specs,
        pl.BlockSpec((None, c, hpg * dh), lambda bi, gi, qi, *_: (bi, qi, gi)),
        jax.ShapeDtypeStruct((b, t, NSA_HEADS * dh), F32), (b, g, t // c),
        c=c, ns=hpg, gr=c, dk=dh + LANES, dv=dh)


def nsa_win(p3, gates3, o_cmp, o_slc, tiles, far, *, c=ATT_C):
    b, t, _ = p3.shape
    g, hpg, dh = NSA_KV_HEADS, NSA_HPG, HEAD_DIM
    kb, vb = 16, 18
    ospec = pl.BlockSpec((None, c, hpg * dh), lambda bi, gi, qi, *_: (bi, qi, gi))
    in_specs = [ospec,
                pl.BlockSpec((None, t, dh), lambda bi, gi, qi, *_: (bi, 0, kb + gi)),
                pl.BlockSpec((None, t, dh), lambda bi, gi, qi, *_: (bi, 0, vb + gi)),
                pl.BlockSpec((hpg, 2, c, c), lambda bi, gi, qi, *_: (gi, 0, 0, 0)),
                pl.BlockSpec((None, c, LANES), lambda bi, gi, qi, *_: (bi, qi, gi)),
                ospec, ospec]
    return _attn_call(
        "win", [far], [p3, p3, p3, tiles, gates3, o_cmp, o_slc], in_specs, ospec,
        jax.ShapeDtypeStruct((b, t, NSA_HEADS * dh), BF16), (b, g, t // c),
        c=c, ns=hpg, gr=c, dk=dh, dv=dh)


def diff_attn(p3, tiles, far, lam, g2, *, c=BIG_C):
    b, t, _ = p3.shape
    c = min(c, t // 2)
    dh = HEAD_DIM
    qb, kb, vb = 20, 28, 36
    in_specs = [pl.BlockSpec((None, c, dh), lambda bi, hi, qi, *_: (bi, qi, qb + hi)),
                pl.BlockSpec((None, t, dh), lambda bi, hi, qi, *_: (bi, 0, kb + hi)),
                pl.BlockSpec((None, t, dh), lambda bi, hi, qi, *_: (bi, 0, vb + hi)),
                pl.BlockSpec((1, 2, c, c), lambda bi, hi, qi, *_: (hi, 0, 0, 0)),
                pl.BlockSpec((1, dh), lambda bi, hi, qi, *_: (0, 0))]
    return _attn_call(
        "diff", [far, lam], [p3, p3, p3, tiles, g2], in_specs,
        pl.BlockSpec((None, c, dh), lambda bi, hi, qi, *_: (bi, qi, hi)),
        jax.ShapeDtypeStruct((b, t, DIFF_HEADS * dh), BF16), (b, DIFF_HEADS, t // c),
        c=c, ns=2, gr=c, dk=dh, dv=dh)


def mla_attn(qf, kf, kv3, *, c=MLA_C):
    b, t, _ = qf.shape
    dk = 2 * HEAD_DIM
    c = min(c, t // 2)
    in_specs = [pl.BlockSpec((None, c, dk), lambda bi, hi, qi: (bi, qi, hi)),
                pl.BlockSpec((None, t, dk), lambda bi, hi, qi: (bi, 0, hi)),
                pl.BlockSpec((None, t, MLA_V), lambda bi, hi, qi: (bi, 0, MLA_HEADS + hi))]
    return _attn_call(
        "mla", [], [qf, kf, kv3], in_specs,
        pl.BlockSpec((None, c, MLA_V), lambda bi, hi, qi: (bi, qi, hi)),
        jax.ShapeDtypeStruct((b, t, MLA_HEADS * MLA_V), BF16), (b, MLA_HEADS, t // c),
        c=c, ns=2, gr=c // 2, dk=dk, dv=MLA_V)


def _mla_assemble_kernel(q_ref, kv_ref, c_ref, cos_ref, sin_ref, qf_ref, kf_ref):
    h16 = MLA_HEADS
    cos = cos_ref[...]
    sin = sin_ref[...]
    lane = lax.broadcasted_iota(I32, cos.shape, 1)
    first = lane < MLA_ROPE
    kpe = c_ref[:, 0:LANES] * cos + c_ref[:, LANES:2 * LANES] * sin
    for j in range(h16 // 2):
        pe = q_ref[:, (h16 + j) * LANES:(h16 + j + 1) * LANES]
        pep = q_ref[:, (h16 + h16 // 2 + j) * LANES:(h16 + h16 // 2 + j + 1) * LANES]
        qpe = pe * cos + pep * sin
        for par in range(2):
            h = 2 * j + par
            keep = first if par == 0 else jnp.logical_not(first)
            qf_ref[:, (2 * h) * LANES:(2 * h + 1) * LANES] = q_ref[:, h * LANES:(h + 1) * LANES].astype(BF16)
            qf_ref[:, (2 * h + 1) * LANES:(2 * h + 2) * LANES] = jnp.where(keep, qpe, 0.0).astype(BF16)
            kf_ref[:, (2 * h) * LANES:(2 * h + 1) * LANES] = kv_ref[:, h * LANES:(h + 1) * LANES].astype(BF16)
            kf_ref[:, (2 * h + 1) * LANES:(2 * h + 2) * LANES] = jnp.where(keep, kpe, 0.0).astype(BF16)


def mla_assemble(qraw, kvraw, craw, cos2, sin2, seq, *, tm=256):
    n = qraw.shape[0]
    tm = min(tm, seq)
    nt = seq // tm
    wq = qraw.shape[1]
    return pl.pallas_call(
        _mla_assemble_kernel,
        grid=(n // tm,),
        in_specs=[pl.BlockSpec((tm, wq), lambda i: (i, 0)),
                  pl.BlockSpec((tm, MLA_HEADS * LANES), lambda i: (i, 0)),
                  pl.BlockSpec((tm, 2 * LANES), lambda i: (i, 4)),
                  pl.BlockSpec((tm, LANES), lambda i: (i % nt, 0)),
                  pl.BlockSpec((tm, LANES), lambda i: (i % nt, 0))],
        out_specs=[pl.BlockSpec((tm, 2 * MLA_HEADS * LANES), lambda i: (i, 0)),
                   pl.BlockSpec((tm, 2 * MLA_HEADS * LANES), lambda i: (i, 0))],
        out_shape=[jax.ShapeDtypeStruct((n, 2 * MLA_HEADS * LANES), BF16)] * 2,
        compiler_params=_cparams(1),
        name="mla_assemble",
    )(qraw, kvraw, craw, cos2, sin2)


def _mem_attn_kernel(q_ref, kv_ref, o_ref):
    hd = MEM_HEAD_DIM
    voff = MEM_HEADS * hd
    for h in range(MEM_HEADS):
        sl = slice(h * hd, (h + 1) * hd)
        s = _dot_nt(q_ref[:, sl], kv_ref[:, sl])
        m = jnp.max(s, axis=-1, keepdims=True)
        p = jnp.exp(s - m)
        p = p / jnp.sum(p, axis=-1, keepdims=True)
        o_ref[:, sl] = _dot(p.astype(BF16), kv_ref[:, voff + h * hd:voff + (h + 1) * hd]).astype(o_ref.dtype)


def mem_attn(q3, kv3, *, tq=512):
    b, t, w = q3.shape
    m = kv3.shape[1]
    tq = min(tq, t)
    return pl.pallas_call(
        _mem_attn_kernel,
        grid=(b, t // tq),
        in_specs=[pl.BlockSpec((None, tq, w), lambda bi, qi: (bi, qi, 0)),
                  pl.BlockSpec((None, m, 2 * w), lambda bi, qi: (bi, 0, 0))],
        out_specs=pl.BlockSpec((None, tq, w), lambda bi, qi: (bi, qi, 0)),
        out_shape=jax.ShapeDtypeStruct((b, t, w), BF16),
        compiler_params=_cparams(2),
        name="mem_attn",
    )(q3, kv3)


def _router_kernel(h_ref, g_ref, rw_ref, rb_ref, hn_ref, ro_ref):
    x = h_ref[...]
    hn = x * lax.rsqrt(jnp.mean(x * x, axis=-1, keepdims=True) + RMS_EPS) * g_ref[...]
    hn_ref[...] = hn
    logits = jnp.dot(hn, rw_ref[...], precision=HIGHEST, preferred_element_type=F32) + rb_ref[...]
    lane = lax.broadcasted_iota(I32, logits.shape, 1)
    lg = jnp.where(lane < N_EXPERTS, logits, -jnp.inf)
    v1 = jnp.max(lg, axis=-1, keepdims=True)
    i1 = jnp.min(jnp.where(lg == v1, lane, LANES), axis=-1, keepdims=True)
    lg2 = jnp.where(lane == i1, -jnp.inf, lg)
    v2 = jnp.max(lg2, axis=-1, keepdims=True)
    i2 = jnp.min(jnp.where(lg2 == v2, lane, LANES), axis=-1, keepdims=True)
    e2 = jnp.exp(v2 - v1)
    w1 = 1.0 / (1.0 + e2)
    w2 = e2 / (1.0 + e2)
    ro_ref[...] = jnp.where(lane == 0, i1.astype(F32),
                            jnp.where(lane == 1, i2.astype(F32),
                                      jnp.where(lane == 2, w1, jnp.where(lane == 3, w2, 0.0))))


def moe_router(h, g, rw, rb, *, tm=256):
    n, d = h.shape
    rwp = jnp.zeros((d, LANES), F32).at[:, :N_EXPERTS].set(rw.astype(F32))
    rbp = jnp.zeros((1, LANES), F32).at[0, :N_EXPERTS].set(rb.astype(F32))
    return pl.pallas_call(
        _router_kernel,
        grid=(n // tm,),
        in_specs=[pl.BlockSpec((tm, d), lambda i: (i, 0)),
                  pl.BlockSpec((1, d), lambda i: (0, 0)),
                  pl.BlockSpec((d, LANES), lambda i: (0, 0)),
                  pl.BlockSpec((1, LANES), lambda i: (0, 0))],
        out_specs=[pl.BlockSpec((tm, d), lambda i: (i, 0)),
                   pl.BlockSpec((tm, LANES), lambda i: (i, 0))],
        out_shape=[jax.ShapeDtypeStruct((n, d), F32), jax.ShapeDtypeStruct((n, LANES), F32)],
        compiler_params=_cparams(1),
        name="moe_router",
    )(h, g.reshape(1, d).astype(F32), rwp, rbp)


def _row_copy(src, dst, sem):
    return pltpu.make_async_copy(src, dst, sem)


def _dispatch_kernel(d1_ref, d2_ref, x_ref, xs_in_ref, xs_ref, sem):
    del xs_in_ref
    tm = x_ref.shape[0]
    base = pl.program_id(0) * tm

    def issue(r, carry):
        src = x_ref.at[pl.ds(r, 1), :]
        _row_copy(src, xs_ref.at[pl.ds(d1_ref[base + r], 1), :], sem).start()
        _row_copy(src, xs_ref.at[pl.ds(d2_ref[base + r], 1), :], sem).start()
        return carry

    def drain(r, carry):
        src = x_ref.at[pl.ds(r, 1), :]
        _row_copy(src, xs_ref.at[pl.ds(0, 1), :], sem).wait()
        _row_copy(src, xs_ref.at[pl.ds(0, 1), :], sem).wait()
        return carry

    lax.fori_loop(0, tm, issue, 0)
    lax.fori_loop(0, tm, drain, 0)


def moe_dispatch(hn, d1, d2, rows, *, tm=256):
    n, d = hn.shape
    return pl.pallas_call(
        _dispatch_kernel,
        grid_spec=pltpu.PrefetchScalarGridSpec(
            num_scalar_prefetch=2, grid=(n // tm,),
            in_specs=[pl.BlockSpec((tm, d), lambda i, *_: (i, 0)),
                      pl.BlockSpec(memory_space=pl.ANY)],
            out_specs=pl.BlockSpec(memory_space=pl.ANY),
            scratch_shapes=[pltpu.SemaphoreType.DMA(())]),
        out_shape=jax.ShapeDtypeStruct((rows, d), hn.dtype),
        input_output_aliases={3: 0},
        compiler_params=_cparams(1),
        name="moe_dispatch",
    )(d1, d2, hn, jnp.zeros((rows, d), hn.dtype))


def _combine_kernel(d1_ref, d2_ref, ys_ref, h_ref, ro_ref, o_ref, b1_ref, b2_ref, sem):
    tm = h_ref.shape[0]
    base = pl.program_id(0) * tm

    def issue(r, carry):
        _row_copy(ys_ref.at[pl.ds(d1_ref[base + r], 1), :], b1_ref.at[pl.ds(r, 1), :], sem).start()
        _row_copy(ys_ref.at[pl.ds(d2_ref[base + r], 1), :], b2_ref.at[pl.ds(r, 1), :], sem).start()
        return carry

    def drain(r, carry):
        _row_copy(ys_ref.at[pl.ds(0, 1), :], b1_ref.at[pl.ds(r, 1), :], sem).wait()
        _row_copy(ys_ref.at[pl.ds(0, 1), :], b2_ref.at[pl.ds(r, 1), :], sem).wait()
        return carry

    lax.fori_loop(0, tm, issue, 0)
    lax.fori_loop(0, tm, drain, 0)
    ro = ro_ref[...]
    o_ref[...] = h_ref[...] + ro[:, 2:3] * b1_ref[...] + ro[:, 3:4] * b2_ref[...]


def moe_combine(ys, h, ro, d1, d2, *, tm=256):
    n, d = h.shape
    return pl.pallas_call(
        _combine_kernel,
        grid_spec=pltpu.PrefetchScalarGridSpec(
            num_scalar_prefetch=2, grid=(n // tm,),
            in_specs=[pl.BlockSpec(memory_space=pl.ANY),
                      pl.BlockSpec((tm, d), lambda i, *_: (i, 0)),
                      pl.BlockSpec((tm, LANES), lambda i, *_: (i, 0))],
            out_specs=pl.BlockSpec((tm, d), lambda i, *_: (i, 0)),
            scratch_shapes=[pltpu.VMEM((tm, d), F32), pltpu.VMEM((tm, d), F32),
                            pltpu.SemaphoreType.DMA(())]),
        out_shape=jax.ShapeDtypeStruct((n, d), F32),
        compiler_params=_cparams(1),
        name="moe_combine",
    )(d1, d2, ys, h, ro)


def moe_block(h, g, rw, rb, w_gu, w_down, layer, *, tm_e=EXPERT_TM):
    n, d = h.shape
    tm_e = min(tm_e, n)
    hn, ro = moe_router(h, g, rw, rb)
    e1 = ro[:, 0].astype(I32)
    e2 = ro[:, 1].astype(I32)
    ind = (jax.nn.one_hot(e1, N_EXPERTS, dtype=I32) + jax.nn.one_hot(e2, N_EXPERTS, dtype=I32))
    csum = jnp.cumsum(ind, axis=0)
    rank = csum - ind
    cnt = csum[-1]
    pcnt = (cnt + tm_e - 1) // tm_e * tm_e
    ends = jnp.cumsum(pcnt)
    off = ends - pcnt
    d1 = (off[e1] + jnp.take_along_axis(rank, e1[:, None], axis=1)[:, 0]).astype(I32)
    d2 = (off[e2] + jnp.take_along_axis(rank, e2[:, None], axis=1)[:, 0]).astype(I32)
    rows = 2 * n + N_EXPERTS * tm_e
    tile_start = jnp.arange(rows // tm_e, dtype=I32) * tm_e
    tile_e = jnp.minimum(jnp.sum((tile_start[:, None] >= ends[None, :]).astype(I32), axis=1),
                         N_EXPERTS - 1).astype(I32)
    xs = moe_dispatch(hn, d1, d2, rows)
    tile_w = tile_e + layer * N_EXPERTS
    n_valid = (ends[-1] // tm_e).astype(I32).reshape(1)
    act = gmm(xs.astype(BF16), w_gu, tile_w, n_valid, tm=tm_e, out_dtype=BF16, swiglu=True)
    ys = gmm(act, w_down, tile_w, n_valid, tm=tm_e, out_dtype=F32)
    return moe_combine(ys, h, ro, d1, d2)


def _mem_block(h, mem2, b, t, i, g_ca, g_mem, wq, wkv, wo):
    n, d = h.shape
    hn = rmsnorm(h, g_ca[i], BF16)
    memn = rmsnorm(mem2, g_mem[i], BF16)
    q = mm(hn, (wq[i] * MEM_HEAD_DIM ** -0.5).astype(BF16), out_dtype=BF16)
    kv = mm(memn, wkv, group=i, out_dtype=BF16)
    o = mem_attn(q.reshape(b, t, -1), kv.reshape(b, mem2.shape[0] // b, -1))
    return mm(o.reshape(n, -1), wo, group=i, out_dtype=F32, res=h)


def _even_layer(h, b, t, li, lam_init, p, tabs):
    n, d = h.shape
    w_in = p["even_w_in"][li]
    nq, nkv = NSA_HEADS * HEAD_DIM, NSA_KV_HEADS * HEAD_DIM
    g0 = nq + 6 * nkv
    dq0 = g0 + 3 * NSA_HEADS
    dqw = DIFF_HEADS * 2 * DIFF_QK_DIM
    w_main = jnp.concatenate([w_in[:, :nq] * HEAD_DIM ** -0.5, w_in[:, nq:g0],
                              w_in[:, dq0:dq0 + dqw] * DIFF_QK_DIM ** -0.5, w_in[:, dq0 + dqw:]],
                             axis=1).astype(BF16)
    w_gate = jnp.zeros((d, NSA_KV_HEADS * LANES), F32)
    for g in range(NSA_KV_HEADS):
        w_gate = w_gate.at[:, g * LANES:g * LANES + 3 * NSA_HPG].set(
            w_in[:, g0 + g * 3 * NSA_HPG:g0 + (g + 1) * 3 * NSA_HPG])
    xn = rmsnorm(h, p["norm_mix"][2 * li], BF16)
    proj = mm(xn, w_main, out_dtype=BF16)
    gates = mm(xn, w_gate.astype(BF16), out_dtype=F32)
    p3 = proj.reshape(b, t, -1)

    kvs = p3[:, :, nq:nq + 2 * nkv].reshape(b, t, 2, NSA_KV_HEADS, HEAD_DIM)
    acat = kvs.transpose(2, 0, 3, 1, 4).reshape(2, b, NSA_KV_HEADS, t // CMP_STRIDE, CMP_STRIDE * HEAD_DIM)
    w1 = p["nsa_cmp_w1"][li]
    half = CMP_STRIDE * HEAD_DIM
    w1cat = jnp.concatenate([w1[:, :half], w1[:, half:]], axis=2).astype(BF16)
    pos = p["nsa_cmp_pos"][li].reshape(2, 2, half)
    pos2 = jnp.zeros((2, 8, half), F32).at[:, :2].set(pos).astype(BF16)
    kvc = nsa_compress(acat, w1cat, p["nsa_cmp_w2"][li].astype(BF16), pos2)
    o_cmp, sel = nsa_cmp_select(p3, kvc, tabs["cbias"], tabs["overlap"])
    k_slc = p3[:, :, nq + 2 * nkv:nq + 3 * nkv].reshape(b, t, NSA_KV_HEADS, HEAD_DIM)
    k_aug = jnp.concatenate([k_slc, jnp.broadcast_to(tabs["blk_onehot"][None, :, None, :],
                                                     (b, t, NSA_KV_HEADS, LANES))], axis=-1)
    o_slc = nsa_slc(p3, k_aug.reshape(b, t, -1), sel, tabs["tiles_a2"], tabs["far_a"])
    o_a = nsa_win(p3, gates.reshape(b, t, -1), o_cmp, o_slc, tabs["tiles_a"], tabs["far_a"])

    lq1, lk1, lq2, lk2 = p["diff_lambda"][li].astype(F32)
    lam = jnp.exp(jnp.sum(lq1 * lk1)) - jnp.exp(jnp.sum(lq2 * lk2)) + lam_init
    g2 = (p["diff_subln"][li].astype(F32) * (1.0 - lam_init)).reshape(1, HEAD_DIM)
    o_b = diff_attn(p3, tabs["tiles_b"], tabs["far_b"], lam.reshape(1), g2)

    mix_in = jnp.concatenate([o_a, o_b], axis=-1).reshape(n, -1)
    h = mm(mix_in, p["even_w_out"], group=li, out_dtype=F32, res=h)
    return h


def _odd_layer(h, b, t, li, p, tabs):
    n, d = h.shape
    hh, nope, rp = MLA_HEADS, MLA_NOPE, MLA_ROPE
    scale = (nope + rp) ** -0.5

    def rot(w):
        w = w.reshape(w.shape[0], -1, 2, rp // 2)
        return jnp.stack([-w[:, :, 1], w[:, :, 0]], axis=2).reshape(w.shape[0], -1)

    w_in = p["odd_w_in"][li]
    kpe_w = w_in[:, MLA_Q_RANK + MLA_KV_RANK:]
    w_in_ext = jnp.concatenate([w_in[:, :MLA_Q_RANK + MLA_KV_RANK], kpe_w, kpe_w, rot(kpe_w), rot(kpe_w)],
                               axis=1).astype(BF16)
    wq = (p["mla_w_q_up"][li] * scale).reshape(MLA_Q_RANK, hh, nope + rp)
    wq_pe = wq[:, :, nope:].reshape(MLA_Q_RANK, hh * rp)
    wq_ext = jnp.concatenate([wq[:, :, :nope].reshape(MLA_Q_RANK, hh * nope), wq_pe, rot(wq_pe)],
                             axis=1).astype(BF16)
    wkv = p["mla_w_kv_up"][li].reshape(MLA_KV_RANK, hh, nope + MLA_V)
    wkv_ext = jnp.concatenate([wkv[:, :, :nope].reshape(MLA_KV_RANK, -1),
                               wkv[:, :, nope:].reshape(MLA_KV_RANK, -1)], axis=1).astype(BF16)

    xn = rmsnorm(h, p["norm_mix"][2 * li + 1], BF16)
    craw = mm(xn, w_in_ext, out_dtype=F32)
    cqn = rmsnorm(craw, p["mla_q_norm"][li], BF16, width=MLA_Q_RANK, col_block=0)
    ckvn = rmsnorm(craw, p["mla_kv_norm"][li], BF16, width=MLA_KV_RANK, col_block=1)
    qraw = mm(cqn, wq_ext, out_dtype=F32)
    kvraw = mm(ckvn, wkv_ext, out_dtype=BF16)
    qf, kf = mla_assemble(qraw, kvraw, craw, tabs["cos2"], tabs["sin2"], t)
    o = mla_attn(qf.reshape(b, t, -1), kf.reshape(b, t, -1), kvraw.reshape(b, t, -1))
    return mm(o.reshape(n, -1), p["odd_w_out"], group=li, out_dtype=F32, res=h)


def _tables(rel_bias, t):
    bias_a, bias_b = rel_bias[:, :NSA_HEADS], rel_bias[:, NSA_HEADS:]
    tiles_a, far_a = _bias_tiles(bias_a, ATT_C)
    tiles_a2, _ = _bias_tiles(bias_a, min(BIG_C, t // 2))
    tiles_b, far_b = _bias_tiles(bias_b, min(BIG_C, t // 2))
    ncb = t // CMP_STRIDE
    nslc = t // SLC_BLOCK
    tt = jnp.arange(t)
    blk_start = jnp.arange(ncb) * CMP_STRIDE
    cbias = _bias_lookup(tt[:, None] - (blk_start + CMP_BLOCK - 1)[None, :], bias_a)
    jj = jnp.arange(nslc)
    n_cmp = (t - CMP_BLOCK) // CMP_STRIDE + 1
    overlap = ((blk_start[:, None] < (jj[None, :] + 1) * SLC_BLOCK)
               & (blk_start[:, None] + CMP_BLOCK > jj[None, :] * SLC_BLOCK)
               & (jnp.arange(ncb)[:, None] < n_cmp)).astype(F32)
    blk_onehot = (jnp.arange(LANES)[None, :] == (tt[:, None] // SLC_BLOCK)).astype(BF16)
    inv_freq = ROPE_THETA ** (-jnp.arange(0, MLA_ROPE, 2, dtype=F32) / MLA_ROPE)
    ang = jnp.arange(t, dtype=F32)[:, None] * inv_freq
    cos2 = jnp.tile(jnp.cos(ang), (1, 2 * LANES // MLA_ROPE))
    sin2 = jnp.tile(jnp.sin(ang), (1, 2 * LANES // MLA_ROPE))
    return dict(tiles_a=tiles_a, tiles_a2=tiles_a2, far_a=far_a, tiles_b=tiles_b, far_b=far_b, cbias=cbias,
                overlap=overlap.T, blk_onehot=blk_onehot, cos2=cos2, sin2=sin2)


def kernel(x, mem, rel_bias, norm_mix, norm_mem, norm_ca, norm_ffn, norm_final, even_w_in, even_w_out, nsa_cmp_pos, nsa_cmp_w1, nsa_cmp_w2, diff_lambda, diff_subln, ffn_w_gu, ffn_w_down, odd_w_in, mla_q_norm, mla_kv_norm, mla_w_q_up, mla_w_kv_up, odd_w_out, router_w, router_b, moe_w_gu, moe_w_down, ca_wq, ca_wkv, ca_wo):
    p = dict(norm_mix=norm_mix, even_w_in=even_w_in, even_w_out=even_w_out, nsa_cmp_pos=nsa_cmp_pos,
             nsa_cmp_w1=nsa_cmp_w1, nsa_cmp_w2=nsa_cmp_w2, diff_lambda=diff_lambda, diff_subln=diff_subln,
             odd_w_in=odd_w_in, mla_q_norm=mla_q_norm, mla_kv_norm=mla_kv_norm, mla_w_q_up=mla_w_q_up,
             mla_w_kv_up=mla_w_kv_up, odd_w_out=odd_w_out)
    b, t, d = x.shape
    depth = norm_mix.shape[0]
    tabs = _tables(rel_bias, t)
    h = x.reshape(b * t, d).astype(F32)
    mem2 = mem.reshape(-1, d).astype(F32)
    for i in range(depth):
        li = i // 2
        if i % 2 == 0:
            h = _even_layer(h, b, t, li, 0.8 - 0.6 * math.exp(-0.3 * i), p, tabs)
        else:
            h = _odd_layer(h, b, t, li, p, tabs)
        h = _mem_block(h, mem2, b, t, i, norm_ca, norm_mem, ca_wq, ca_wkv, ca_wo)
        if i % 2 == 0:
            hn = rmsnorm(h, norm_ffn[i], BF16)
            act = mm(hn, ffn_w_gu, group=li, out_dtype=BF16, swiglu=True)
            h = mm(act, ffn_w_down, group=li, out_dtype=F32, res=h)
        else:
            h = moe_block(h, norm_ffn[i], router_w[li], router_b[li],
                          moe_w_gu.reshape((-1,) + moe_w_gu.shape[2:]),
                          moe_w_down.reshape((-1,) + moe_w_down.shape[2:]), li)
    return rmsnorm(h, norm_final, x.dtype).reshape(b, t, d)
```

```python
import functools
import math

import jax
import jax.numpy as jnp
from jax import lax
from jax.experimental import pallas as pl
from jax.experimental.pallas import tpu as pltpu

F32 = jnp.float32
BF16 = jnp.bfloat16
I32 = jnp.int32

HEAD_DIM = 128
NSA_HEADS = 8
NSA_KV_HEADS = 2
NSA_HPG = NSA_HEADS // NSA_KV_HEADS
CMP_BLOCK = 32
CMP_STRIDE = 16
SLC_BLOCK = 64
SLC_TOPK = 16
WINDOW = 512
DIFF_HEADS = 8
DIFF_QK_DIM = 64
MLA_HEADS = 16
MLA_Q_RANK = 512
MLA_KV_RANK = 512
MLA_NOPE = 128
MLA_ROPE = 64
MLA_V = 128
ROPE_THETA = 10000.0
MEM_HEADS = 4
MEM_HEAD_DIM = 128
N_EXPERTS = 8
NUM_BUCKETS = 32
MAX_DISTANCE = 128
RMS_EPS = 1e-6
NEG_INF = -1e30
BIG = 1e30

LANES = 128
MXU_COLS = 256
GMM_VMEM_BUDGET = 46 * 1024 * 1024
VMEM_LIMIT = 56 * 1024 * 1024
ATT_C = 256
BIG_C = 512
MLA_C = 512
EXPERT_TM = 512
HIGHEST = lax.Precision.HIGHEST


def _cparams(n_axes):
    return pltpu.CompilerParams(dimension_semantics=("arbitrary",) * n_axes,
                                vmem_limit_bytes=VMEM_LIMIT)


def _dot(a, b):
    return jnp.dot(a, b, preferred_element_type=F32)


def _rep(x, n):
    return x if n == 1 else jnp.concatenate([x] * n, axis=1)


def _rep_cat(xs):
    return xs[0] if len(xs) == 1 else jnp.concatenate(xs, axis=1)


def _dot_nt(a, b):
    return lax.dot_general(a, b, (((1,), (1,)), ((), ())), preferred_element_type=F32)


def _rms_kernel(x_ref, g_ref, o_ref):
    x = x_ref[...].astype(F32)
    y = x * lax.rsqrt(jnp.mean(x * x, axis=-1, keepdims=True) + RMS_EPS)
    o_ref[...] = (y * g_ref[...]).astype(o_ref.dtype)


def rmsnorm(x, g, out_dtype, width=None, col_block=0, tm=512):
    m = x.shape[0]
    width = width or x.shape[1]
    tm = min(tm, m)
    return pl.pallas_call(
        _rms_kernel,
        grid=(m // tm,),
        in_specs=[pl.BlockSpec((tm, width), lambda i: (i, col_block)),
                  pl.BlockSpec((1, width), lambda i: (0, 0))],
        out_specs=pl.BlockSpec((tm, width), lambda i: (i, 0)),
        out_shape=jax.ShapeDtypeStruct((m, width), out_dtype),
        compiler_params=_cparams(1),
        name="rmsnorm",
    )(x, g.reshape(1, width).astype(F32))


def _unpack_halves(xp):
    hi = lax.bitcast_convert_type(xp & jnp.uint32(0xFFFF0000), F32).astype(BF16)
    lo = lax.bitcast_convert_type(xp << 16, F32).astype(BF16)
    return hi, lo


def _pack_halves(x):
    xb = x.astype(BF16).astype(F32)
    half = x.shape[1] // 2
    hi = lax.bitcast_convert_type(xb[:, :half], jnp.uint32)
    lo = lax.bitcast_convert_type(xb[:, half:], jnp.uint32)
    return hi | (lo >> 16)


def _gmm_kernel(te_ref, nv_ref, x_ref, w_ref, *rest, swiglu, has_res, cast_w, packed_x):
    nw = 2 if swiglu else 1
    w_refs = (w_ref,) + tuple(rest[:nw - 1])
    res_ref = rest[nw - 1] if has_res else None
    o_ref = rest[nw - 1 + int(has_res)]
    i = pl.program_id(1)
    if cast_w:
        wb_refs = rest[nw + int(has_res):]
        prev = te_ref[jnp.maximum(i - 1, 0)]

        @pl.when((i == 0) | (te_ref[i] != prev))
        def _():
            for src, dst in zip(w_refs, wb_refs):
                dst[...] = src[...].astype(BF16)
        w_refs = wb_refs

    @pl.when(i < nv_ref[0])
    def _():
        if packed_x:
            xh, xl = _unpack_halves(x_ref[...])
            half = xh.shape[1]

            def xdot(w):
                return _dot(xh, w[:half]) + _dot(xl, w[half:])
        else:
            x = x_ref[...].astype(BF16)

            def xdot(w):
                return _dot(x, w[...])
        acc = xdot(w_refs[0])
        if swiglu:
            acc = acc * jax.nn.sigmoid(acc) * xdot(w_refs[1])
        if has_res:
            acc = acc + res_ref[...]
        o_ref[...] = acc.astype(o_ref.dtype)

    @pl.when(i >= nv_ref[0])
    def _():
        o_ref[...] = jnp.zeros(o_ref.shape, o_ref.dtype)


def _gmm_tn(k, n, tm, x_bytes, w_bytes, out_bytes, n_w, has_res):
    best = None
    for tn in range(LANES, n + 1, LANES):
        if n % tn or (tn % MXU_COLS and tn != n):
            continue
        fp = (n_w * k * tn * w_bytes * 2 + (n_w * k * tn * 2 if w_bytes == 4 else 0)
              + tm * k * x_bytes * 2 + tm * tn * out_bytes * 2 + tm * tn * 4 * n_w
              + (tm * tn * 4 * 2 if has_res else 0))
        if fp <= GMM_VMEM_BUDGET:
            best = tn
    assert best is not None, (k, n, tm)
    return best


def gmm(x, w, tile_e, n_valid, *, tm, out_dtype, swiglu=False, res=None, packed_x=False):
    r, kx = x.shape
    k = 2 * kx if packed_x else kx
    n = w.shape[2] // 2 if swiglu else w.shape[2]
    n_w = 2 if swiglu else 1
    cast_w = w.dtype != BF16
    tn = _gmm_tn(k, n, tm, x.dtype.itemsize * kx // k, w.dtype.itemsize, jnp.dtype(out_dtype).itemsize,
                 n_w, res is not None)
    nj = n // tn
    in_specs = [pl.BlockSpec((tm, kx), lambda j, i, te, nv: (jnp.minimum(i, nv[0] - 1), 0)),
                pl.BlockSpec((None, k, tn), lambda j, i, te, nv: (te[i], 0, j))]
    args = [x, w]
    if swiglu:
        in_specs.append(pl.BlockSpec((None, k, tn), lambda j, i, te, nv: (te[i], 0, j + nj)))
        args.append(w)
    if res is not None:
        in_specs.append(pl.BlockSpec((tm, tn), lambda j, i, te, nv: (i, j)))
        args.append(res)
    scratch = [pltpu.VMEM((k, tn), BF16)] * (n_w if cast_w else 0)
    return pl.pallas_call(
        functools.partial(_gmm_kernel, swiglu=swiglu, has_res=res is not None, cast_w=cast_w,
                          packed_x=packed_x),
        grid_spec=pltpu.PrefetchScalarGridSpec(
            num_scalar_prefetch=2,
            grid=(nj, r // tm),
            in_specs=in_specs,
            out_specs=pl.BlockSpec((tm, tn), lambda j, i, te, nv: (i, j)),
            scratch_shapes=scratch),
        out_shape=jax.ShapeDtypeStruct((r, n), out_dtype),
        compiler_params=_cparams(2),
        name="gmm_swiglu" if swiglu else "gmm",
    )(tile_e, n_valid, *args)


def mm(x, w, *, out_dtype, tm=512, swiglu=False, res=None, group=0):
    m = x.shape[0]
    tm = min(tm, m)
    if w.ndim == 2:
        w = w[None]
    return gmm(x, w, jnp.full((m // tm,), group, I32), jnp.full((1,), m // tm, I32), tm=tm,
               out_dtype=out_dtype, swiglu=swiglu, res=res)


def _t5_bucket(dist):
    n = jnp.maximum(dist, 0)
    max_exact = NUM_BUCKETS // 2
    nf = jnp.maximum(n, 1).astype(F32)
    large = max_exact + (jnp.log(nf / max_exact) / math.log(MAX_DISTANCE / max_exact)
                         * (NUM_BUCKETS - max_exact)).astype(I32)
    large = jnp.minimum(large, NUM_BUCKETS - 1)
    return jnp.where(n < max_exact, n, large)


def _bias_lookup(dist, tbl):
    oh = jax.nn.one_hot(_t5_bucket(dist), NUM_BUCKETS, dtype=F32)
    return jnp.einsum("...k,kh->h...", oh, tbl.astype(F32), precision=HIGHEST)


def _bias_tiles(tbl, c):
    i = jnp.arange(c)[:, None]
    j = jnp.arange(c)[None, :]
    tiles = jnp.stack([_bias_lookup(i - j, tbl), _bias_lookup(c + i - j, tbl)], axis=1)
    far = _bias_lookup(jnp.asarray(2 * c), tbl)
    assert c + 1 >= MAX_DISTANCE
    return tiles, far


def _compress_kernel(a_ref, w1_ref, w2_ref, pos_ref, o_ref):
    nblk = a_ref.shape[0]
    pq = _dot(a_ref[...], w1_ref[...])
    r = _dot(pos_ref[...], w1_ref[...])
    c = r[0:1, :HEAD_DIM] + r[1:2, HEAD_DIM:]
    hp = pq[:, :HEAD_DIM] + pltpu.roll(pq[:, HEAD_DIM:], nblk - 1, 0) + c
    hdn = jax.nn.gelu(hp)
    o_ref[...] = _dot(hdn.astype(BF16), w2_ref[...]).astype(o_ref.dtype)


def nsa_compress(acat, w1cat, w2, pos2):
    _, b, g, nblk, wid = acat.shape
    return pl.pallas_call(
        _compress_kernel,
        grid=(2, b, g),
        in_specs=[pl.BlockSpec((None, None, None, nblk, wid), lambda s, bi, gi: (s, bi, gi, 0, 0)),
                  pl.BlockSpec((None, wid, 2 * HEAD_DIM), lambda s, bi, gi: (s, 0, 0)),
                  pl.BlockSpec((None, HEAD_DIM, HEAD_DIM), lambda s, bi, gi: (s, 0, 0)),
                  pl.BlockSpec((None, 8, wid), lambda s, bi, gi: (s, 0, 0))],
        out_specs=pl.BlockSpec((None, None, None, nblk, HEAD_DIM), lambda s, bi, gi: (s, bi, gi, 0, 0)),
        out_shape=jax.ShapeDtypeStruct((2, b, g, nblk, HEAD_DIM), BF16),
        compiler_params=_cparams(3),
        name="nsa_compress",
    )(acat, w1cat, w2, pos2)


def _nsa_cmp_kernel(q_ref, kc_ref, vc_ref, cb_ref, ov_ref, o_ref, sel_ref, sc_ref, *, tq, topk):
    qi = pl.program_id(2)
    ncb = kc_ref.shape[0]
    nslc = ov_ref.shape[0]
    t = qi * tq + lax.broadcasted_iota(I32, (tq, ncb), 0)
    n = lax.broadcasted_iota(I32, (tq, ncb), 1)
    mask = n * CMP_STRIDE + (CMP_BLOCK - 1) <= t
    kc = kc_ref[...]
    vc = vc_ref[...]
    psum = jnp.zeros((tq, ncb), F32)
    for h in range(NSA_HPG):
        qh = q_ref[:, h * HEAD_DIM:(h + 1) * HEAD_DIM]
        s = jnp.where(mask, _dot_nt(qh, kc) + cb_ref[h], NEG_INF)
        m = jnp.max(s, axis=-1, keepdims=True)
        p = jnp.where(mask, jnp.exp(s - m), 0.0)
        p = p / jnp.maximum(jnp.sum(p, axis=-1, keepdims=True), 1e-30)
        o_ref[:, h * HEAD_DIM:(h + 1) * HEAD_DIM] = _dot(p.astype(BF16), vc).astype(o_ref.dtype)
        psum = psum + p
    imp = lax.dot_general(ov_ref[...], psum, (((1,), (1,)), ((), ())), precision=HIGHEST,
                          preferred_element_type=F32)
    tt = qi * tq + lax.broadcasted_iota(I32, (nslc, tq), 1)
    j = lax.broadcasted_iota(I32, (nslc, tq), 0)
    cur = tt // SLC_BLOCK
    forced = (j == 0) | (j == cur) | (j == cur - 1)
    visible = j * SLC_BLOCK <= tt
    sc_ref[...] = jnp.where(forced, BIG, jnp.where(visible, imp, NEG_INF))
    score = sc_ref[...]
    rank = jnp.zeros((nslc, tq), F32)
    for i in range(nslc):
        other = sc_ref[i:i + 1, :]
        tie = jnp.where(j > i, 1.0, 0.0)
        rank = rank + jnp.where(other > score, 1.0, jnp.where(other == score, tie, 0.0))
    neg = jnp.where(rank < topk, 0.0, NEG_INF)
    neg = jnp.concatenate([neg, jnp.zeros((LANES - nslc, tq), F32)], axis=0)
    sel_ref[...] = neg.T.astype(sel_ref.dtype)


def nsa_cmp_select(p3, kvc, cbias, overlap, *, tq=ATT_C):
    b, t, _ = p3.shape
    g = NSA_KV_HEADS
    ncb = kvc.shape[3]
    nslc = t // SLC_BLOCK
    assert nslc <= LANES and nslc % 8 == 0
    gw = NSA_HPG * HEAD_DIM
    return pl.pallas_call(
        functools.partial(_nsa_cmp_kernel, tq=tq, topk=min(SLC_TOPK, nslc)),
        grid=(b, g, t // tq),
        in_specs=[pl.BlockSpec((None, tq, gw), lambda bi, gi, qi: (bi, qi, gi)),
                  pl.BlockSpec((None, None, None, ncb, HEAD_DIM), lambda bi, gi, qi: (0, bi, gi, 0, 0)),
                  pl.BlockSpec((None, None, None, ncb, HEAD_DIM), lambda bi, gi, qi: (1, bi, gi, 0, 0)),
                  pl.BlockSpec((NSA_HPG, tq, ncb), lambda bi, gi, qi: (gi, qi, 0)),
                  pl.BlockSpec((nslc, ncb), lambda bi, gi, qi: (0, 0))],
        out_specs=[pl.BlockSpec((None, tq, gw), lambda bi, gi, qi: (bi, qi, gi)),
                   pl.BlockSpec((None, None, tq, LANES), lambda bi, gi, qi: (bi, gi, qi, 0))],
        out_shape=[jax.ShapeDtypeStruct((b, t, NSA_HEADS * HEAD_DIM), F32),
                   jax.ShapeDtypeStruct((b, g, t, LANES), BF16)],
        scratch_shapes=[pltpu.VMEM((nslc, tq), F32)],
        compiler_params=_cparams(3),
        name="nsa_cmp_select",
    )(p3, kvc, kvc, cbias, overlap)


def _attn_kernel(*refs, cfg):
    kind = cfg["kind"]
    c = cfg["c"]
    ns = cfg["ns"]
    has_bias = kind != "mla"
    it = iter(refs)
    far_ref = next(it) if has_bias else None
    lam_ref = next(it) if kind == "diff" else None
    q_ref, k_ref, v_ref = next(it), next(it), next(it)
    bt_ref = next(it) if has_bias else None
    sel_ref = next(it) if kind == "slc" else None
    if kind == "win":
        gl_ref, ocmp_ref, oslc_ref = next(it), next(it), next(it)
    if kind == "diff":
        g2_ref = next(it)
    o_ref = next(it)
    qs_ref = next(it) if kind != "mla" else None
    m_ref, acc_ref, s_ref = next(it), next(it), next(it)

    hg = pl.program_id(1)
    qi = pl.program_id(2)
    dv = acc_ref.shape[1] - LANES

    gr = m_ref.shape[0] // ns

    if kind in ("slc", "win"):
        for h in range(ns):
            qs_ref[h * gr:(h + 1) * gr, 0:HEAD_DIM] = q_ref[:, h * HEAD_DIM:(h + 1) * HEAD_DIM]
            if kind == "slc":
                qs_ref[h * gr:(h + 1) * gr, HEAD_DIM:] = sel_ref[...]
    elif kind == "diff":
        q = q_ref[...]
        lane = lax.broadcasted_iota(I32, q.shape, 1)
        qs_ref[0:gr, :] = jnp.where(lane < DIFF_QK_DIM, q, jnp.zeros_like(q))
        qs_ref[gr:2 * gr, :] = jnp.where(lane >= DIFF_QK_DIM, q, jnp.zeros_like(q))
    else:
        qs_ref = q_ref
    m_ref[...] = jnp.full(m_ref.shape, NEG_INF, F32)
    acc_ref[...] = jnp.zeros(acc_ref.shape, F32)

    row = lax.broadcasted_iota(I32, (gr, c), 0)
    col = lax.broadcasted_iota(I32, (gr, c), 1)
    assert c % gr == 0 and (gr == c or kind == "mla")

    sw = min(c, ATT_C)

    def qk(k0, w):
        k0 = pl.multiple_of(k0, c)
        kc = k_ref[pl.ds(k0, w), :]
        for h in range(ns):
            qh = qs_ref[h * gr:(h + 1) * gr, :]
            for u in range(w // sw):
                s_ref[h * gr:(h + 1) * gr, u * sw:(u + 1) * sw] = _dot_nt(qh, kc[u * sw:(u + 1) * sw])

    def consume(k0, modes):
        k0 = pl.multiple_of(k0, c)
        w = len(modes) * c
        vc = v_ref[pl.ds(k0, w), :]
        v_ext = jnp.concatenate([vc, jnp.ones((w, LANES), vc.dtype)], axis=1)
        nsub = w // sw
        uniform = all(md in ("far", "edge") for md in modes)
        for h in range(ns):
            r0, r1 = h * gr, (h + 1) * gr
            hb = hg * ns + h if kind in ("slc", "win") else hg
            ht = h if kind in ("slc", "win") else 0
            shift = None
            if uniform:
                s = s_ref[r0:r1, 0:w]
                if has_bias:
                    shift = far_ref[hb]
                if modes[0] == "edge":
                    s = jnp.where(col > row, s, NEG_INF)
            else:
                parts = []
                for u, md in enumerate(modes):
                    su = s_ref[r0:r1, u * c:(u + 1) * c]
                    if has_bias:
                        su = su + (far_ref[hb] if md == "far" else bt_ref[ht, 0 if md == "diag" else 1])
                    if md == "diag":
                        su = jnp.where(row + (h * gr if kind == "mla" else 0) >= col, su, NEG_INF)
                    parts.append(su)
                s = _rep_cat(parts)
            mx = jnp.max(s, axis=-1, keepdims=True)
            if shift is not None:
                mx = mx + shift
            m_prev = m_ref[r0:r1]
            m_new = jnp.maximum(m_prev, mx)
            alpha = jnp.exp(m_prev - m_new)
            sub = m_new if shift is None else m_new - shift
            p = jnp.exp(s - _rep(sub, w // LANES))
            if modes[0] == "edge":
                p = jnp.where(col > row, p, 0.0)
            pb = p.astype(BF16)
            pv = _dot(pb[:, :sw], v_ext[:sw])
            for u in range(1, nsub):
                pv = pv + _dot(pb[:, u * sw:(u + 1) * sw], v_ext[u * sw:(u + 1) * sw])
            acc_ref[r0:r1] = _rep(alpha, acc_ref.shape[1] // LANES) * acc_ref[r0:r1] + pv
            m_ref[r0:r1] = m_new

    if kind == "win":
        nback = WINDOW // c
        assert nback * c == WINDOW and nback >= 2
        for back in range(nback, 0, -1):
            mode = "edge" if back == nback else ("near" if back == 1 else "far")

            @pl.when(qi - back >= 0)
            def _(back=back, mode=mode):
                qk((qi - back) * c, c)
                consume((qi - back) * c, (mode,))
        qk(qi * c, c)
        consume(qi * c, ("diag",))
    else:
        n_far = jnp.maximum(qi - 1, 0)
        lead = (n_far % 2) * c

        @pl.when(n_far % 2 == 1)
        def _():
            qk(0, c)
            consume(0, ("far",))

        qk(lead, 2 * c)

        def far_body(j, carry):
            consume(lead + j * (2 * c), ("far", "far"))
            qk(lead + (j + 1) * (2 * c), 2 * c)
            return carry
        lax.fori_loop(0, n_far // 2, far_body, 0)

        @pl.when(qi >= 1)
        def _():
            consume((qi - 1) * c, ("near", "diag"))

        @pl.when(qi == 0)
        def _():
            consume(0, ("diag",))

    def out_stream(h):
        r0, r1 = h * gr, (h + 1) * gr
        return acc_ref[r0:r1, :dv] / jnp.maximum(acc_ref[r0:r1, dv:], 1e-30)

    if kind == "slc":
        for h in range(ns):
            o_ref[:, h * dv:(h + 1) * dv] = out_stream(h).astype(o_ref.dtype)
    elif kind == "win":
        gate = jax.nn.sigmoid(gl_ref[...])
        for h in range(ns):
            sl = slice(h * dv, (h + 1) * dv)
            o = (gate[:, 3 * h:3 * h + 1] * ocmp_ref[:, sl]
                 + gate[:, 3 * h + 1:3 * h + 2] * oslc_ref[:, sl]
                 + gate[:, 3 * h + 2:3 * h + 3] * out_stream(h))
            o_ref[:, sl] = o.astype(o_ref.dtype)
    elif kind == "diff":
        o = out_stream(0) - lam_ref[0] * out_stream(1)
        y = o * lax.rsqrt(jnp.mean(o * o, axis=-1, keepdims=True) + RMS_EPS)
        o_ref[...] = (y * g2_ref[...]).astype(o_ref.dtype)
    else:
        for h in range(ns):
            o_ref[h * gr:(h + 1) * gr, :] = out_stream(h).astype(o_ref.dtype)


def _attn_call(kind, scalars, tensors, in_specs, out_spec, out_shape, grid, *, c, ns, gr, dk, dv):
    scratch = []
    if kind != "mla":
        scratch.append(pltpu.VMEM((ns * gr, dk), BF16))
    assert dv == LANES
    sw_cols = c if kind == "win" else 2 * c
    scratch += [pltpu.VMEM((ns * gr, LANES), F32), pltpu.VMEM((ns * gr, dv + LANES), F32),
                pltpu.VMEM((ns * gr, sw_cols), F32)]
    assert kind == "win" or grid[2] >= 2
    cfg = dict(kind=kind, c=c, ns=ns)
    return pl.pallas_call(
        functools.partial(_attn_kernel, cfg=cfg),
        grid_spec=pltpu.PrefetchScalarGridSpec(
            num_scalar_prefetch=len(scalars), grid=grid, in_specs=in_specs,
            out_specs=out_spec, scratch_shapes=scratch),
        out_shape=out_shape,
        compiler_params=_cparams(3),
        name="attn_" + kind,
    )(*scalars, *tensors)


def nsa_slc(p3, k_aug, sel, tiles, far, *, c=BIG_C):
    b, t, _ = p3.shape
    c = min(c, t // 2)
    g, hpg, dh = NSA_KV_HEADS, NSA_HPG, HEAD_DIM
    vb = 14
    in_specs = [pl.BlockSpec((None, c, hpg * dh), lambda bi, gi, qi, *_: (bi, qi, gi)),
                pl.BlockSpec((None, t, dh + LANES), lambda bi, gi, qi, *_: (bi, 0, gi)),
                pl.BlockSpec((None, t, dh), lambda bi, gi, qi, *_: (bi, 0, vb + gi)),
                pl.BlockSpec((hpg, 2, c, c), lambda bi, gi, qi, *_: (gi, 0, 0, 0)),
                pl.BlockSpec((None, None, c, LANES), lambda bi, gi, qi, *_: (bi, gi, qi, 0))]
    return _attn_call(
        "slc", [far], [p3, k_aug, p3, tiles, sel], in_specs,
        pl.BlockSpec((None, c, hpg * dh), lambda bi, gi, qi, *_: (bi, qi, gi)),
        jax.ShapeDtypeStruct((b, t, NSA_HEADS * dh), F32), (b, g, t // c),
        c=c, ns=hpg, gr=c, dk=dh + LANES, dv=dh)


def nsa_win(p3, gates3, o_cmp, o_slc, tiles, far, *, c=ATT_C):
    b, t, _ = p3.shape
    g, hpg, dh = NSA_KV_HEADS, NSA_HPG, HEAD_DIM
    kb, vb = 16, 18
    ospec = pl.BlockSpec((None, c, hpg * dh), lambda bi, gi, qi, *_: (bi, qi, gi))
    in_specs = [ospec,
                pl.BlockSpec((None, t, dh), lambda bi, gi, qi, *_: (bi, 0, kb + gi)),
                pl.BlockSpec((None, t, dh), lambda bi, gi, qi, *_: (bi, 0, vb + gi)),
                pl.BlockSpec((hpg, 2, c, c), lambda bi, gi, qi, *_: (gi, 0, 0, 0)),
                pl.BlockSpec((None, c, LANES), lambda bi, gi, qi, *_: (bi, qi, gi)),
                ospec, ospec]
    return _attn_call(
        "win", [far], [p3, p3, p3, tiles, gates3, o_cmp, o_slc], in_specs, ospec,
        jax.ShapeDtypeStruct((b, t, NSA_HEADS * dh), BF16), (b, g, t // c),
        c=c, ns=hpg, gr=c, dk=dh, dv=dh)


def diff_attn(p3, tiles, far, lam, g2, *, c=BIG_C):
    b, t, _ = p3.shape
    c = min(c, t // 2)
    dh = HEAD_DIM
    qb, kb, vb = 20, 28, 36
    in_specs = [pl.BlockSpec((None, c, dh), lambda bi, hi, qi, *_: (bi, qi, qb + hi)),
                pl.BlockSpec((None, t, dh), lambda bi, hi, qi, *_: (bi, 0, kb + hi)),
                pl.BlockSpec((None, t, dh), lambda bi, hi, qi, *_: (bi, 0, vb + hi)),
                pl.BlockSpec((1, 2, c, c), lambda bi, hi, qi, *_: (hi, 0, 0, 0)),
                pl.BlockSpec((1, dh), lambda bi, hi, qi, *_: (0, 0))]
    return _attn_call(
        "diff", [far, lam], [p3, p3, p3, tiles, g2], in_specs,
        pl.BlockSpec((None, c, dh), lambda bi, hi, qi, *_: (bi, qi, hi)),
        jax.ShapeDtypeStruct((b, t, DIFF_HEADS * dh), BF16), (b, DIFF_HEADS, t // c),
        c=c, ns=2, gr=c, dk=dh, dv=dh)


def mla_attn(qf, kf, kv3, *, c=MLA_C):
    b, t, _ = qf.shape
    dk = 2 * HEAD_DIM
    c = min(c, t // 2)
    in_specs = [pl.BlockSpec((None, c, dk), lambda bi, hi, qi: (bi, qi, hi)),
                pl.BlockSpec((None, t, dk), lambda bi, hi, qi: (bi, 0, hi)),
                pl.BlockSpec((None, t, MLA_V), lambda bi, hi, qi: (bi, 0, MLA_HEADS + hi))]
    return _attn_call(
        "mla", [], [qf, kf, kv3], in_specs,
        pl.BlockSpec((None, c, MLA_V), lambda bi, hi, qi: (bi, qi, hi)),
        jax.ShapeDtypeStruct((b, t, MLA_HEADS * MLA_V), BF16), (b, MLA_HEADS, t // c),
        c=c, ns=2, gr=c // 2, dk=dk, dv=MLA_V)


def _mla_assemble_kernel(q_ref, kv_ref, c_ref, cos_ref, sin_ref, qf_ref, kf_ref):
    h16 = MLA_HEADS
    cos = cos_ref[...]
    sin = sin_ref[...]
    lane = lax.broadcasted_iota(I32, cos.shape, 1)
    first = lane < MLA_ROPE
    kpe = c_ref[:, 0:LANES] * cos + c_ref[:, LANES:2 * LANES] * sin
    for j in range(h16 // 2):
        pe = q_ref[:, (h16 + j) * LANES:(h16 + j + 1) * LANES]
        pep = q_ref[:, (h16 + h16 // 2 + j) * LANES:(h16 + h16 // 2 + j + 1) * LANES]
        qpe = pe * cos + pep * sin
        for par in range(2):
            h = 2 * j + par
            keep = first if par == 0 else jnp.logical_not(first)
            qf_ref[:, (2 * h) * LANES:(2 * h + 1) * LANES] = q_ref[:, h * LANES:(h + 1) * LANES].astype(BF16)
            qf_ref[:, (2 * h + 1) * LANES:(2 * h + 2) * LANES] = jnp.where(keep, qpe, 0.0).astype(BF16)
            kf_ref[:, (2 * h) * LANES:(2 * h + 1) * LANES] = kv_ref[:, h * LANES:(h + 1) * LANES].astype(BF16)
            kf_ref[:, (2 * h + 1) * LANES:(2 * h + 2) * LANES] = jnp.where(keep, kpe, 0.0).astype(BF16)


def mla_assemble(qraw, kvraw, craw, cos2, sin2, seq, *, tm=256):
    n = qraw.shape[0]
    tm = min(tm, seq)
    nt = seq // tm
    wq = qraw.shape[1]
    return pl.pallas_call(
        _mla_assemble_kernel,
        grid=(n // tm,),
        in_specs=[pl.BlockSpec((tm, wq), lambda i: (i, 0)),
                  pl.BlockSpec((tm, MLA_HEADS * LANES), lambda i: (i, 0)),
                  pl.BlockSpec((tm, 2 * LANES), lambda i: (i, 4)),
                  pl.BlockSpec((tm, LANES), lambda i: (i % nt, 0)),
                  pl.BlockSpec((tm, LANES), lambda i: (i % nt, 0))],
        out_specs=[pl.BlockSpec((tm, 2 * MLA_HEADS * LANES), lambda i: (i, 0)),
                   pl.BlockSpec((tm, 2 * MLA_HEADS * LANES), lambda i: (i, 0))],
        out_shape=[jax.ShapeDtypeStruct((n, 2 * MLA_HEADS * LANES), BF16)] * 2,
        compiler_params=_cparams(1),
        name="mla_assemble",
    )(qraw, kvraw, craw, cos2, sin2)


def _mem_attn_kernel(q_ref, kv_ref, o_ref):
    hd = MEM_HEAD_DIM
    voff = MEM_HEADS * hd
    for h in range(MEM_HEADS):
        sl = slice(h * hd, (h + 1) * hd)
        s = _dot_nt(q_ref[:, sl], kv_ref[:, sl])
        m = jnp.max(s, axis=-1, keepdims=True)
        p = jnp.exp(s - m)
        p = p / jnp.sum(p, axis=-1, keepdims=True)
        o_ref[:, sl] = _dot(p.astype(BF16), kv_ref[:, voff + h * hd:voff + (h + 1) * hd]).astype(o_ref.dtype)


def mem_attn(q3, kv3, *, tq=512):
    b, t, w = q3.shape
    m = kv3.shape[1]
    tq = min(tq, t)
    return pl.pallas_call(
        _mem_attn_kernel,
        grid=(b, t // tq),
        in_specs=[pl.BlockSpec((None, tq, w), lambda bi, qi: (bi, qi, 0)),
                  pl.BlockSpec((None, m, 2 * w), lambda bi, qi: (bi, 0, 0))],
        out_specs=pl.BlockSpec((None, tq, w), lambda bi, qi: (bi, qi, 0)),
        out_shape=jax.ShapeDtypeStruct((b, t, w), BF16),
        compiler_params=_cparams(2),
        name="mem_attn",
    )(q3, kv3)


def _router_kernel(h_ref, g_ref, rw_ref, rb_ref, hn_ref, ro_ref):
    x = h_ref[...]
    hn = x * lax.rsqrt(jnp.mean(x * x, axis=-1, keepdims=True) + RMS_EPS) * g_ref[...]
    hn_ref[...] = _pack_halves(hn)
    logits = jnp.dot(hn, rw_ref[...], precision=HIGHEST, preferred_element_type=F32) + rb_ref[...]
    lane = lax.broadcasted_iota(I32, logits.shape, 1)
    lg = jnp.where(lane < N_EXPERTS, logits, -jnp.inf)
    v1 = jnp.max(lg, axis=-1, keepdims=True)
    i1 = jnp.min(jnp.where(lg == v1, lane, LANES), axis=-1, keepdims=True)
    lg2 = jnp.where(lane == i1, -jnp.inf, lg)
    v2 = jnp.max(lg2, axis=-1, keepdims=True)
    i2 = jnp.min(jnp.where(lg2 == v2, lane, LANES), axis=-1, keepdims=True)
    e2 = jnp.exp(v2 - v1)
    w1 = 1.0 / (1.0 + e2)
    w2 = e2 / (1.0 + e2)
    ro_ref[...] = jnp.where(lane == 0, i1.astype(F32),
                            jnp.where(lane == 1, i2.astype(F32),
                                      jnp.where(lane == 2, w1, jnp.where(lane == 3, w2, 0.0))))


def moe_router(h, g, rw, rb, *, tm=256):
    n, d = h.shape
    rwp = jnp.zeros((d, LANES), F32).at[:, :N_EXPERTS].set(rw.astype(F32))
    rbp = jnp.zeros((1, LANES), F32).at[0, :N_EXPERTS].set(rb.astype(F32))
    return pl.pallas_call(
        _router_kernel,
        grid=(n // tm,),
        in_specs=[pl.BlockSpec((tm, d), lambda i: (i, 0)),
                  pl.BlockSpec((1, d), lambda i: (0, 0)),
                  pl.BlockSpec((d, LANES), lambda i: (0, 0)),
                  pl.BlockSpec((1, LANES), lambda i: (0, 0))],
        out_specs=[pl.BlockSpec((tm, d // 2), lambda i: (i, 0)),
                   pl.BlockSpec((tm, LANES), lambda i: (i, 0))],
        out_shape=[jax.ShapeDtypeStruct((n, d // 2), jnp.uint32), jax.ShapeDtypeStruct((n, LANES), F32)],
        compiler_params=_cparams(1),
        name="moe_router",
    )(h, g.reshape(1, d).astype(F32), rwp, rbp)


def _row_copy(src, dst, sem):
    return pltpu.make_async_copy(src, dst, sem)


def _dispatch_kernel(d1_ref, d2_ref, x_ref, xs_in_ref, xs_ref, sem):
    del xs_in_ref
    tm = x_ref.shape[0]
    base = pl.program_id(0) * tm

    def issue(r, carry):
        src = x_ref.at[pl.ds(r, 1), :]
        _row_copy(src, xs_ref.at[pl.ds(d1_ref[base + r], 1), :], sem).start()
        _row_copy(src, xs_ref.at[pl.ds(d2_ref[base + r], 1), :], sem).start()
        return carry

    def drain(r, carry):
        src = x_ref.at[pl.ds(r, 1), :]
        _row_copy(src, xs_ref.at[pl.ds(0, 1), :], sem).wait()
        _row_copy(src, xs_ref.at[pl.ds(0, 1), :], sem).wait()
        return carry

    lax.fori_loop(0, tm, issue, 0)
    lax.fori_loop(0, tm, drain, 0)


def moe_dispatch(hn, d1, d2, rows, *, tm=256):
    n, d = hn.shape
    return pl.pallas_call(
        _dispatch_kernel,
        grid_spec=pltpu.PrefetchScalarGridSpec(
            num_scalar_prefetch=2, grid=(n // tm,),
            in_specs=[pl.BlockSpec((tm, d), lambda i, *_: (i, 0)),
                      pl.BlockSpec(memory_space=pl.ANY)],
            out_specs=pl.BlockSpec(memory_space=pl.ANY),
            scratch_shapes=[pltpu.SemaphoreType.DMA(())]),
        out_shape=jax.ShapeDtypeStruct((rows, d), hn.dtype),
        input_output_aliases={3: 0},
        compiler_params=_cparams(1),
        name="moe_dispatch",
    )(d1, d2, hn, jnp.zeros((rows, d), hn.dtype))


def _combine_kernel(d1_ref, d2_ref, ys_ref, h_ref, ro_ref, o_ref, b1_ref, b2_ref, sem):
    tm = h_ref.shape[0]
    base = pl.program_id(0) * tm

    def issue(r, carry):
        _row_copy(ys_ref.at[pl.ds(d1_ref[base + r], 1), :], b1_ref.at[pl.ds(r, 1), :], sem).start()
        _row_copy(ys_ref.at[pl.ds(d2_ref[base + r], 1), :], b2_ref.at[pl.ds(r, 1), :], sem).start()
        return carry

    def drain(r, carry):
        _row_copy(ys_ref.at[pl.ds(0, 1), :], b1_ref.at[pl.ds(r, 1), :], sem).wait()
        _row_copy(ys_ref.at[pl.ds(0, 1), :], b2_ref.at[pl.ds(r, 1), :], sem).wait()
        return carry

    lax.fori_loop(0, tm, issue, 0)
    lax.fori_loop(0, tm, drain, 0)
    ro = ro_ref[...]
    o_ref[...] = h_ref[...] + ro[:, 2:3] * b1_ref[...] + ro[:, 3:4] * b2_ref[...]


def moe_combine(ys, h, ro, d1, d2, *, tm=256):
    n, d = h.shape
    return pl.pallas_call(
        _combine_kernel,
        grid_spec=pltpu.PrefetchScalarGridSpec(
            num_scalar_prefetch=2, grid=(n // tm,),
            in_specs=[pl.BlockSpec(memory_space=pl.ANY),
                      pl.BlockSpec((tm, d), lambda i, *_: (i, 0)),
                      pl.BlockSpec((tm, LANES), lambda i, *_: (i, 0))],
            out_specs=pl.BlockSpec((tm, d), lambda i, *_: (i, 0)),
            scratch_shapes=[pltpu.VMEM((tm, d), F32), pltpu.VMEM((tm, d), F32),
                            pltpu.SemaphoreType.DMA(())]),
        out_shape=jax.ShapeDtypeStruct((n, d), F32),
        compiler_params=_cparams(1),
        name="moe_combine",
    )(d1, d2, ys, h, ro)


def moe_block(h, g, rw, rb, w_gu, w_down, layer, *, tm_e=EXPERT_TM):
    n, d = h.shape
    tm_e = min(tm_e, n)
    hn, ro = moe_router(h, g, rw, rb)
    e1 = ro[:, 0].astype(I32)
    e2 = ro[:, 1].astype(I32)
    ind = (jax.nn.one_hot(e1, N_EXPERTS, dtype=I32) + jax.nn.one_hot(e2, N_EXPERTS, dtype=I32))
    csum = jnp.cumsum(ind, axis=0)
    rank = csum - ind
    cnt = csum[-1]
    pcnt = (cnt + tm_e - 1) // tm_e * tm_e
    ends = jnp.cumsum(pcnt)
    off = ends - pcnt
    d1 = (off[e1] + jnp.take_along_axis(rank, e1[:, None], axis=1)[:, 0]).astype(I32)
    d2 = (off[e2] + jnp.take_along_axis(rank, e2[:, None], axis=1)[:, 0]).astype(I32)
    rows = 2 * n + N_EXPERTS * tm_e
    tile_start = jnp.arange(rows // tm_e, dtype=I32) * tm_e
    tile_e = jnp.minimum(jnp.sum((tile_start[:, None] >= ends[None, :]).astype(I32), axis=1),
                         N_EXPERTS - 1).astype(I32)
    xs = moe_dispatch(hn, d1, d2, rows)
    tile_w = tile_e + layer * N_EXPERTS
    n_valid = (ends[-1] // tm_e).astype(I32).reshape(1)
    act = gmm(xs, w_gu, tile_w, n_valid, tm=tm_e, out_dtype=BF16, swiglu=True, packed_x=True)
    ys = gmm(act, w_down, tile_w, n_valid, tm=tm_e, out_dtype=F32)
    return moe_combine(ys, h, ro, d1, d2)


def _mem_block(h, mem2, b, t, i, g_ca, g_mem, wq, wkv, wo):
    n, d = h.shape
    hn = rmsnorm(h, g_ca[i], BF16)
    memn = rmsnorm(mem2, g_mem[i], BF16)
    q = mm(hn, (wq[i] * MEM_HEAD_DIM ** -0.5).astype(BF16), out_dtype=BF16)
    kv = mm(memn, wkv, group=i, out_dtype=BF16)
    o = mem_attn(q.reshape(b, t, -1), kv.reshape(b, mem2.shape[0] // b, -1))
    return mm(o.reshape(n, -1), wo, group=i, out_dtype=F32, res=h)


def _even_layer(h, b, t, li, lam_init, p, tabs):
    n, d = h.shape
    w_in = p["even_w_in"][li]
    nq, nkv = NSA_HEADS * HEAD_DIM, NSA_KV_HEADS * HEAD_DIM
    g0 = nq + 6 * nkv
    dq0 = g0 + 3 * NSA_HEADS
    dqw = DIFF_HEADS * 2 * DIFF_QK_DIM
    w_main = jnp.concatenate([w_in[:, :nq] * HEAD_DIM ** -0.5, w_in[:, nq:g0],
                              w_in[:, dq0:dq0 + dqw] * DIFF_QK_DIM ** -0.5, w_in[:, dq0 + dqw:]],
                             axis=1).astype(BF16)
    w_gate = jnp.zeros((d, NSA_KV_HEADS * LANES), F32)
    for g in range(NSA_KV_HEADS):
        w_gate = w_gate.at[:, g * LANES:g * LANES + 3 * NSA_HPG].set(
            w_in[:, g0 + g * 3 * NSA_HPG:g0 + (g + 1) * 3 * NSA_HPG])
    xn = rmsnorm(h, p["norm_mix"][2 * li], BF16)
    proj = mm(xn, w_main, out_dtype=BF16)
    gates = mm(xn, w_gate.astype(BF16), out_dtype=F32)
    p3 = proj.reshape(b, t, -1)

    kvs = p3[:, :, nq:nq + 2 * nkv].reshape(b, t, 2, NSA_KV_HEADS, HEAD_DIM)
    acat = kvs.transpose(2, 0, 3, 1, 4).reshape(2, b, NSA_KV_HEADS, t // CMP_STRIDE, CMP_STRIDE * HEAD_DIM)
    w1 = p["nsa_cmp_w1"][li]
    half = CMP_STRIDE * HEAD_DIM
    w1cat = jnp.concatenate([w1[:, :half], w1[:, half:]], axis=2).astype(BF16)
    pos = p["nsa_cmp_pos"][li].reshape(2, 2, half)
    pos2 = jnp.zeros((2, 8, half), F32).at[:, :2].set(pos).astype(BF16)
    kvc = nsa_compress(acat, w1cat, p["nsa_cmp_w2"][li].astype(BF16), pos2)
    o_cmp, sel = nsa_cmp_select(p3, kvc, tabs["cbias"], tabs["overlap"])
    k_slc = p3[:, :, nq + 2 * nkv:nq + 3 * nkv].reshape(b, t, NSA_KV_HEADS, HEAD_DIM)
    k_aug = jnp.concatenate([k_slc, jnp.broadcast_to(tabs["blk_onehot"][None, :, None, :],
                                                     (b, t, NSA_KV_HEADS, LANES))], axis=-1)
    o_slc = nsa_slc(p3, k_aug.reshape(b, t, -1), sel, tabs["tiles_a2"], tabs["far_a"])
    o_a = nsa_win(p3, gates.reshape(b, t, -1), o_cmp, o_slc, tabs["tiles_a"], tabs["far_a"])

    lq1, lk1, lq2, lk2 = p["diff_lambda"][li].astype(F32)
    lam = jnp.exp(jnp.sum(lq1 * lk1)) - jnp.exp(jnp.sum(lq2 * lk2)) + lam_init
    g2 = (p["diff_subln"][li].astype(F32) * (1.0 - lam_init)).reshape(1, HEAD_DIM)
    o_b = diff_attn(p3, tabs["tiles_b"], tabs["far_b"], lam.reshape(1), g2)

    mix_in = jnp.concatenate([o_a, o_b], axis=-1).reshape(n, -1)
    h = mm(mix_in, p["even_w_out"], group=li, out_dtype=F32, res=h)
    return h


def _odd_layer(h, b, t, li, p, tabs):
    n, d = h.shape
    hh, nope, rp = MLA_HEADS, MLA_NOPE, MLA_ROPE
    scale = (nope + rp) ** -0.5

    def rot(w):
        w = w.reshape(w.shape[0], -1, 2, rp // 2)
        return jnp.stack([-w[:, :, 1], w[:, :, 0]], axis=2).reshape(w.shape[0], -1)

    w_in = p["odd_w_in"][li]
    kpe_w = w_in[:, MLA_Q_RANK + MLA_KV_RANK:]
    w_in_ext = jnp.concatenate([w_in[:, :MLA_Q_RANK + MLA_KV_RANK], kpe_w, kpe_w, rot(kpe_w), rot(kpe_w)],
                               axis=1).astype(BF16)
    wq = (p["mla_w_q_up"][li] * scale).reshape(MLA_Q_RANK, hh, nope + rp)
    wq_pe = wq[:, :, nope:].reshape(MLA_Q_RANK, hh * rp)
    wq_ext = jnp.concatenate([wq[:, :, :nope].reshape(MLA_Q_RANK, hh * nope), wq_pe, rot(wq_pe)],
                             axis=1).astype(BF16)
    wkv = p["mla_w_kv_up"][li].reshape(MLA_KV_RANK, hh, nope + MLA_V)
    wkv_ext = jnp.concatenate([wkv[:, :, :nope].reshape(MLA_KV_RANK, -1),
                               wkv[:, :, nope:].reshape(MLA_KV_RANK, -1)], axis=1).astype(BF16)

    xn = rmsnorm(h, p["norm_mix"][2 * li + 1], BF16)
    craw = mm(xn, w_in_ext, out_dtype=F32)
    cqn = rmsnorm(craw, p["mla_q_norm"][li], BF16, width=MLA_Q_RANK, col_block=0)
    ckvn = rmsnorm(craw, p["mla_kv_norm"][li], BF16, width=MLA_KV_RANK, col_block=1)
    qraw = mm(cqn, wq_ext, out_dtype=F32)
    kvraw = mm(ckvn, wkv_ext, out_dtype=BF16)
    qf, kf = mla_assemble(qraw, kvraw, craw, tabs["cos2"], tabs["sin2"], t)
    o = mla_attn(qf.reshape(b, t, -1), kf.reshape(b, t, -1), kvraw.reshape(b, t, -1))
    return mm(o.reshape(n, -1), p["odd_w_out"], group=li, out_dtype=F32, res=h)


def _tables(rel_bias, t):
    bias_a, bias_b = rel_bias[:, :NSA_HEADS], rel_bias[:, NSA_HEADS:]
    tiles_a, far_a = _bias_tiles(bias_a, ATT_C)
    tiles_a2, _ = _bias_tiles(bias_a, min(BIG_C, t // 2))
    tiles_b, far_b = _bias_tiles(bias_b, min(BIG_C, t // 2))
    ncb = t // CMP_STRIDE
    nslc = t // SLC_BLOCK
    tt = jnp.arange(t)
    blk_start = jnp.arange(ncb) * CMP_STRIDE
    cbias = _bias_lookup(tt[:, None] - (blk_start + CMP_BLOCK - 1)[None, :], bias_a)
    jj = jnp.arange(nslc)
    n_cmp = (t - CMP_BLOCK) // CMP_STRIDE + 1
    overlap = ((blk_start[:, None] < (jj[None, :] + 1) * SLC_BLOCK)
               & (blk_start[:, None] + CMP_BLOCK > jj[None, :] * SLC_BLOCK)
               & (jnp.arange(ncb)[:, None] < n_cmp)).astype(F32)
    blk_onehot = (jnp.arange(LANES)[None, :] == (tt[:, None] // SLC_BLOCK)).astype(BF16)
    inv_freq = ROPE_THETA ** (-jnp.arange(0, MLA_ROPE, 2, dtype=F32) / MLA_ROPE)
    ang = jnp.arange(t, dtype=F32)[:, None] * inv_freq
    cos2 = jnp.tile(jnp.cos(ang), (1, 2 * LANES // MLA_ROPE))
    sin2 = jnp.tile(jnp.sin(ang), (1, 2 * LANES // MLA_ROPE))
    return dict(tiles_a=tiles_a, tiles_a2=tiles_a2, far_a=far_a, tiles_b=tiles_b, far_b=far_b, cbias=cbias,
                overlap=overlap.T, blk_onehot=blk_onehot, cos2=cos2, sin2=sin2)


def kernel(x, mem, rel_bias, norm_mix, norm_mem, norm_ca, norm_ffn, norm_final, even_w_in, even_w_out, nsa_cmp_pos, nsa_cmp_w1, nsa_cmp_w2, diff_lambda, diff_subln, ffn_w_gu, ffn_w_down, odd_w_in, mla_q_norm, mla_kv_norm, mla_w_q_up, mla_w_kv_up, odd_w_out, router_w, router_b, moe_w_gu, moe_w_down, ca_wq, ca_wkv, ca_wo):
    p = dict(norm_mix=norm_mix, even_w_in=even_w_in, even_w_out=even_w_out, nsa_cmp_pos=nsa_cmp_pos,
             nsa_cmp_w1=nsa_cmp_w1, nsa_cmp_w2=nsa_cmp_w2, diff_lambda=diff_lambda, diff_subln=diff_subln,
             odd_w_in=odd_w_in, mla_q_norm=mla_q_norm, mla_kv_norm=mla_kv_norm, mla_w_q_up=mla_w_q_up,
             mla_w_kv_up=mla_w_kv_up, odd_w_out=odd_w_out)
    b, t, d = x.shape
    depth = norm_mix.shape[0]
    tabs = _tables(rel_bias, t)
    h = x.reshape(b * t, d).astype(F32)
    mem2 = mem.reshape(-1, d).astype(F32)
    for i in range(depth):
        li = i // 2
        if i % 2 == 0:
            h = _even_layer(h, b, t, li, 0.8 - 0.6 * math.exp(-0.3 * i), p, tabs)
        else:
            h = _odd_layer(h, b, t, li, p, tabs)
        h = _mem_block(h, mem2, b, t, i, norm_ca, norm_mem, ca_wq, ca_wkv, ca_wo)
        if i % 2 == 0:
            hn = rmsnorm(h, norm_ffn[i], BF16)
            act = mm(hn, ffn_w_gu, group=li, out_dtype=BF16, swiglu=True)
            h = mm(act, ffn_w_down, group=li, out_dtype=F32, res=h)
        else:
            h = moe_block(h, norm_ffn[i], router_w[li], router_b[li],
                          moe_w_gu.reshape((-1,) + moe_w_gu.shape[2:]),
                          moe_w_down.reshape((-1,) + moe_w_down.shape[2:]), li)
    return rmsnorm(h, norm_final, x.dtype).reshape(b, t, d)
```

```python
import functools
import math

import jax
import jax.numpy as jnp
from jax import lax
from jax.experimental import pallas as pl
from jax.experimental.pallas import tpu as pltpu

F32 = jnp.float32
BF16 = jnp.bfloat16
I32 = jnp.int32

HEAD_DIM = 128
NSA_HEADS = 8
NSA_KV_HEADS = 2
NSA_HPG = NSA_HEADS // NSA_KV_HEADS
CMP_BLOCK = 32
CMP_STRIDE = 16
SLC_BLOCK = 64
SLC_TOPK = 16
WINDOW = 512
DIFF_HEADS = 8
DIFF_QK_DIM = 64
MLA_HEADS = 16
MLA_Q_RANK = 512
MLA_KV_RANK = 512
MLA_NOPE = 128
MLA_ROPE = 64
MLA_V = 128
ROPE_THETA = 10000.0
MEM_HEADS = 4
MEM_HEAD_DIM = 128
N_EXPERTS = 8
NUM_BUCKETS = 32
MAX_DISTANCE = 128
RMS_EPS = 1e-6
NEG_INF = -1e30
BIG = 1e30

LANES = 128
MXU_COLS = 256
GMM_VMEM_BUDGET = 46 * 1024 * 1024
VMEM_LIMIT = 56 * 1024 * 1024
ATT_C = 256
BIG_C = 512
MLA_C = 512
EXPERT_TM = 512
HIGHEST = lax.Precision.HIGHEST


def _cparams(n_axes):
    return pltpu.CompilerParams(dimension_semantics=("arbitrary",) * n_axes,
                                vmem_limit_bytes=VMEM_LIMIT)


def _dot(a, b):
    return jnp.dot(a, b, preferred_element_type=F32)


def _rep(x, n):
    return x if n == 1 else jnp.concatenate([x] * n, axis=1)


def _rep_cat(xs):
    return xs[0] if len(xs) == 1 else jnp.concatenate(xs, axis=1)


def _dot_nt(a, b):
    return lax.dot_general(a, b, (((1,), (1,)), ((), ())), preferred_element_type=F32)


def _rms_kernel(x_ref, g_ref, o_ref):
    x = x_ref[...].astype(F32)
    y = x * lax.rsqrt(jnp.mean(x * x, axis=-1, keepdims=True) + RMS_EPS)
    o_ref[...] = (y * g_ref[...]).astype(o_ref.dtype)


def rmsnorm(x, g, out_dtype, width=None, col_block=0, tm=512):
    m = x.shape[0]
    width = width or x.shape[1]
    tm = min(tm, m)
    return pl.pallas_call(
        _rms_kernel,
        grid=(m // tm,),
        in_specs=[pl.BlockSpec((tm, width), lambda i: (i, col_block)),
                  pl.BlockSpec((1, width), lambda i: (0, 0))],
        out_specs=pl.BlockSpec((tm, width), lambda i: (i, 0)),
        out_shape=jax.ShapeDtypeStruct((m, width), out_dtype),
        compiler_params=_cparams(1),
        name="rmsnorm",
    )(x, g.reshape(1, width).astype(F32))


def _unpack_halves(xp):
    hi = lax.bitcast_convert_type(xp & jnp.uint32(0xFFFF0000), F32).astype(BF16)
    lo = lax.bitcast_convert_type(xp << 16, F32).astype(BF16)
    return hi, lo


def _pack_halves(x):
    xb = x.astype(BF16).astype(F32)
    half = x.shape[1] // 2
    hi = lax.bitcast_convert_type(xb[:, :half], jnp.uint32)
    lo = lax.bitcast_convert_type(xb[:, half:], jnp.uint32)
    return hi | (lo >> 16)


def _gmm_kernel(te_ref, nv_ref, x_ref, *rest, swiglu, has_res, cast_w, packed_x, norm_x):
    g_ref = rest[0] if norm_x else None
    w_ref, rest = rest[int(norm_x)], rest[int(norm_x) + 1:]
    nw = 2 if swiglu else 1
    w_refs = (w_ref,) + tuple(rest[:nw - 1])
    res_ref = rest[nw - 1] if has_res else None
    o_ref = rest[nw - 1 + int(has_res)]
    i = pl.program_id(1)
    if cast_w:
        wb_refs = rest[nw + int(has_res):]
        prev = te_ref[jnp.maximum(i - 1, 0)]

        @pl.when((i == 0) | (te_ref[i] != prev))
        def _():
            for src, dst in zip(w_refs, wb_refs):
                dst[...] = src[...].astype(BF16)
        w_refs = wb_refs

    @pl.when(i < nv_ref[0])
    def _():
        if packed_x:
            xh, xl = _unpack_halves(x_ref[...])
            half = xh.shape[1]

            def xdot(w):
                return _dot(xh, w[:half]) + _dot(xl, w[half:])
        else:
            x = x_ref[...]
            if norm_x:
                x = x * lax.rsqrt(jnp.mean(x * x, axis=-1, keepdims=True) + RMS_EPS) * g_ref[...]
            x = x.astype(BF16)

            def xdot(w):
                return _dot(x, w[...])
        acc = xdot(w_refs[0])
        if swiglu:
            acc = acc * jax.nn.sigmoid(acc) * xdot(w_refs[1])
        if has_res:
            acc = acc + res_ref[...]
        o_ref[...] = acc.astype(o_ref.dtype)

    @pl.when(i >= nv_ref[0])
    def _():
        o_ref[...] = jnp.zeros(o_ref.shape, o_ref.dtype)


def _gmm_tn(k, n, tm, x_bytes, w_bytes, out_bytes, n_w, has_res):
    best = None
    for tn in range(LANES, n + 1, LANES):
        if n % tn or (tn % MXU_COLS and tn != n):
            continue
        fp = (n_w * k * tn * w_bytes * 2 + (n_w * k * tn * 2 if w_bytes == 4 else 0)
              + tm * k * x_bytes * 2 + tm * tn * out_bytes * 2 + tm * tn * 4 * n_w
              + (tm * tn * 4 * 2 if has_res else 0))
        if fp <= GMM_VMEM_BUDGET:
            best = tn
    assert best is not None, (k, n, tm)
    return best


def gmm(x, w, tile_e, n_valid, *, tm, out_dtype, swiglu=False, res=None, packed_x=False, norm_g=None):
    r, kx = x.shape
    k = 2 * kx if packed_x else kx
    n = w.shape[2] // 2 if swiglu else w.shape[2]
    n_w = 2 if swiglu else 1
    cast_w = w.dtype != BF16
    tn = _gmm_tn(k, n, tm, x.dtype.itemsize * kx // k, w.dtype.itemsize, jnp.dtype(out_dtype).itemsize,
                 n_w, res is not None)
    nj = n // tn
    in_specs = [pl.BlockSpec((tm, kx), lambda j, i, te, nv: (jnp.minimum(i, nv[0] - 1), 0)),
                pl.BlockSpec((None, k, tn), lambda j, i, te, nv: (te[i], 0, j))]
    args = [x, w]
    if norm_g is not None:
        in_specs.insert(1, pl.BlockSpec((1, k), lambda j, i, te, nv: (0, 0)))
        args.insert(1, norm_g.reshape(1, k).astype(F32))
    if swiglu:
        in_specs.append(pl.BlockSpec((None, k, tn), lambda j, i, te, nv: (te[i], 0, j + nj)))
        args.append(w)
    if res is not None:
        in_specs.append(pl.BlockSpec((tm, tn), lambda j, i, te, nv: (i, j)))
        args.append(res)
    scratch = [pltpu.VMEM((k, tn), BF16)] * (n_w if cast_w else 0)
    return pl.pallas_call(
        functools.partial(_gmm_kernel, swiglu=swiglu, has_res=res is not None, cast_w=cast_w,
                          packed_x=packed_x, norm_x=norm_g is not None),
        grid_spec=pltpu.PrefetchScalarGridSpec(
            num_scalar_prefetch=2,
            grid=(nj, r // tm),
            in_specs=in_specs,
            out_specs=pl.BlockSpec((tm, tn), lambda j, i, te, nv: (i, j)),
            scratch_shapes=scratch),
        out_shape=jax.ShapeDtypeStruct((r, n), out_dtype),
        compiler_params=_cparams(2),
        name="gmm_swiglu" if swiglu else "gmm",
    )(tile_e, n_valid, *args)


def mm(x, w, *, out_dtype, tm=512, swiglu=False, res=None, group=0, norm_g=None):
    m = x.shape[0]
    tm = min(tm, m)
    if w.ndim == 2:
        w = w[None]
    return gmm(x, w, jnp.full((m // tm,), group, I32), jnp.full((1,), m // tm, I32), tm=tm,
               out_dtype=out_dtype, swiglu=swiglu, res=res, norm_g=norm_g)


def _t5_bucket(dist):
    n = jnp.maximum(dist, 0)
    max_exact = NUM_BUCKETS // 2
    nf = jnp.maximum(n, 1).astype(F32)
    large = max_exact + (jnp.log(nf / max_exact) / math.log(MAX_DISTANCE / max_exact)
                         * (NUM_BUCKETS - max_exact)).astype(I32)
    large = jnp.minimum(large, NUM_BUCKETS - 1)
    return jnp.where(n < max_exact, n, large)


def _bias_lookup(dist, tbl):
    oh = jax.nn.one_hot(_t5_bucket(dist), NUM_BUCKETS, dtype=F32)
    return jnp.einsum("...k,kh->h...", oh, tbl.astype(F32), precision=HIGHEST)


def _bias_tiles(tbl, c):
    i = jnp.arange(c)[:, None]
    j = jnp.arange(c)[None, :]
    tiles = jnp.stack([_bias_lookup(i - j, tbl), _bias_lookup(c + i - j, tbl)], axis=1)
    far = _bias_lookup(jnp.asarray(2 * c), tbl)
    assert c + 1 >= MAX_DISTANCE
    return tiles, far


def _compress_kernel(a_ref, w1_ref, w2_ref, pos_ref, o_ref):
    nblk = a_ref.shape[0]
    pq = _dot(a_ref[...], w1_ref[...])
    r = _dot(pos_ref[...], w1_ref[...])
    c = r[0:1, :HEAD_DIM] + r[1:2, HEAD_DIM:]
    hp = pq[:, :HEAD_DIM] + pltpu.roll(pq[:, HEAD_DIM:], nblk - 1, 0) + c
    hdn = jax.nn.gelu(hp)
    o_ref[...] = _dot(hdn.astype(BF16), w2_ref[...]).astype(o_ref.dtype)


def nsa_compress(acat, w1cat, w2, pos2):
    _, b, g, nblk, wid = acat.shape
    return pl.pallas_call(
        _compress_kernel,
        grid=(2, b, g),
        in_specs=[pl.BlockSpec((None, None, None, nblk, wid), lambda s, bi, gi: (s, bi, gi, 0, 0)),
                  pl.BlockSpec((None, wid, 2 * HEAD_DIM), lambda s, bi, gi: (s, 0, 0)),
                  pl.BlockSpec((None, HEAD_DIM, HEAD_DIM), lambda s, bi, gi: (s, 0, 0)),
                  pl.BlockSpec((None, 8, wid), lambda s, bi, gi: (s, 0, 0))],
        out_specs=pl.BlockSpec((None, None, None, nblk, HEAD_DIM), lambda s, bi, gi: (s, bi, gi, 0, 0)),
        out_shape=jax.ShapeDtypeStruct((2, b, g, nblk, HEAD_DIM), BF16),
        compiler_params=_cparams(3),
        name="nsa_compress",
    )(acat, w1cat, w2, pos2)


def _nsa_cmp_kernel(q_ref, kc_ref, vc_ref, cb_ref, ov_ref, o_ref, sel_ref, sc_ref, *, tq, topk):
    qi = pl.program_id(2)
    ncb = kc_ref.shape[0]
    nslc = ov_ref.shape[0]
    t = qi * tq + lax.broadcasted_iota(I32, (tq, ncb), 0)
    n = lax.broadcasted_iota(I32, (tq, ncb), 1)
    mask = n * CMP_STRIDE + (CMP_BLOCK - 1) <= t
    kc = kc_ref[...]
    vc = vc_ref[...]
    psum = jnp.zeros((tq, ncb), F32)
    for h in range(NSA_HPG):
        qh = q_ref[:, h * HEAD_DIM:(h + 1) * HEAD_DIM]
        s = jnp.where(mask, _dot_nt(qh, kc) + cb_ref[h], NEG_INF)
        m = jnp.max(s, axis=-1, keepdims=True)
        p = jnp.where(mask, jnp.exp(s - m), 0.0)
        p = p / jnp.maximum(jnp.sum(p, axis=-1, keepdims=True), 1e-30)
        o_ref[:, h * HEAD_DIM:(h + 1) * HEAD_DIM] = _dot(p.astype(BF16), vc).astype(o_ref.dtype)
        psum = psum + p
    imp = lax.dot_general(ov_ref[...], psum, (((1,), (1,)), ((), ())), precision=HIGHEST,
                          preferred_element_type=F32)
    tt = qi * tq + lax.broadcasted_iota(I32, (nslc, tq), 1)
    j = lax.broadcasted_iota(I32, (nslc, tq), 0)
    cur = tt // SLC_BLOCK
    forced = (j == 0) | (j == cur) | (j == cur - 1)
    visible = j * SLC_BLOCK <= tt
    sc_ref[...] = jnp.where(forced, BIG, jnp.where(visible, imp, NEG_INF))
    score = sc_ref[...]
    rank = jnp.zeros((nslc, tq), F32)
    for i in range(nslc):
        other = sc_ref[i:i + 1, :]
        tie = jnp.where(j > i, 1.0, 0.0)
        rank = rank + jnp.where(other > score, 1.0, jnp.where(other == score, tie, 0.0))
    neg = jnp.where(rank < topk, 0.0, NEG_INF)
    neg = jnp.concatenate([neg, jnp.zeros((LANES - nslc, tq), F32)], axis=0)
    sel_ref[...] = neg.T.astype(sel_ref.dtype)


def nsa_cmp_select(p3, kvc, cbias, overlap, *, tq=ATT_C):
    b, t, _ = p3.shape
    g = NSA_KV_HEADS
    ncb = kvc.shape[3]
    nslc = t // SLC_BLOCK
    assert nslc <= LANES and nslc % 8 == 0
    gw = NSA_HPG * HEAD_DIM
    return pl.pallas_call(
        functools.partial(_nsa_cmp_kernel, tq=tq, topk=min(SLC_TOPK, nslc)),
        grid=(b, g, t // tq),
        in_specs=[pl.BlockSpec((None, tq, gw), lambda bi, gi, qi: (bi, qi, gi)),
                  pl.BlockSpec((None, None, None, ncb, HEAD_DIM), lambda bi, gi, qi: (0, bi, gi, 0, 0)),
                  pl.BlockSpec((None, None, None, ncb, HEAD_DIM), lambda bi, gi, qi: (1, bi, gi, 0, 0)),
                  pl.BlockSpec((NSA_HPG, tq, ncb), lambda bi, gi, qi: (gi, qi, 0)),
                  pl.BlockSpec((nslc, ncb), lambda bi, gi, qi: (0, 0))],
        out_specs=[pl.BlockSpec((None, tq, gw), lambda bi, gi, qi: (bi, qi, gi)),
                   pl.BlockSpec((None, None, tq, LANES), lambda bi, gi, qi: (bi, gi, qi, 0))],
        out_shape=[jax.ShapeDtypeStruct((b, t, NSA_HEADS * HEAD_DIM), F32),
                   jax.ShapeDtypeStruct((b, g, t, LANES), BF16)],
        scratch_shapes=[pltpu.VMEM((nslc, tq), F32)],
        compiler_params=_cparams(3),
        name="nsa_cmp_select",
    )(p3, kvc, kvc, cbias, overlap)


def _attn_kernel(*refs, cfg):
    kind = cfg["kind"]
    c = cfg["c"]
    ns = cfg["ns"]
    has_bias = kind != "mla"
    it = iter(refs)
    far_ref = next(it) if has_bias else None
    lam_ref = next(it) if kind == "diff" else None
    q_ref, k_ref, v_ref = next(it), next(it), next(it)
    bt_ref = next(it) if has_bias else None
    sel_ref = next(it) if kind == "slc" else None
    if kind == "win":
        gl_ref, ocmp_ref, oslc_ref = next(it), next(it), next(it)
    if kind == "diff":
        g2_ref = next(it)
    o_ref = next(it)
    qs_ref = next(it) if kind != "mla" else None
    m_ref, acc_ref, s_ref = next(it), next(it), next(it)

    hg = pl.program_id(1)
    qi = pl.program_id(2)
    dv = acc_ref.shape[1] - LANES

    gr = m_ref.shape[0] // ns

    if kind in ("slc", "win"):
        for h in range(ns):
            qs_ref[h * gr:(h + 1) * gr, 0:HEAD_DIM] = q_ref[:, h * HEAD_DIM:(h + 1) * HEAD_DIM]
            if kind == "slc":
                qs_ref[h * gr:(h + 1) * gr, HEAD_DIM:] = sel_ref[...]
    elif kind == "diff":
        q = q_ref[...]
        lane = lax.broadcasted_iota(I32, q.shape, 1)
        qs_ref[0:gr, :] = jnp.where(lane < DIFF_QK_DIM, q, jnp.zeros_like(q))
        qs_ref[gr:2 * gr, :] = jnp.where(lane >= DIFF_QK_DIM, q, jnp.zeros_like(q))
    else:
        qs_ref = q_ref
    m_ref[...] = jnp.full(m_ref.shape, NEG_INF, F32)
    acc_ref[...] = jnp.zeros(acc_ref.shape, F32)

    row = lax.broadcasted_iota(I32, (gr, c), 0)
    col = lax.broadcasted_iota(I32, (gr, c), 1)
    assert c % gr == 0 and (gr == c or kind == "mla")

    sw = min(c, ATT_C)

    def qk(k0, w):
        k0 = pl.multiple_of(k0, c)
        kc = k_ref[pl.ds(k0, w), :]
        for h in range(ns):
            qh = qs_ref[h * gr:(h + 1) * gr, :]
            for u in range(w // sw):
                s_ref[h * gr:(h + 1) * gr, u * sw:(u + 1) * sw] = _dot_nt(qh, kc[u * sw:(u + 1) * sw])

    def consume(k0, modes):
        k0 = pl.multiple_of(k0, c)
        w = len(modes) * c
        vc = v_ref[pl.ds(k0, w), :]
        v_ext = jnp.concatenate([vc, jnp.ones((w, LANES), vc.dtype)], axis=1)
        nsub = w // sw
        uniform = all(md in ("far", "edge") for md in modes)
        for h in range(ns):
            r0, r1 = h * gr, (h + 1) * gr
            hb = hg * ns + h if kind in ("slc", "win") else hg
            ht = h if kind in ("slc", "win") else 0
            shift = None
            if uniform:
                s = s_ref[r0:r1, 0:w]
                if has_bias:
                    shift = far_ref[hb]
                if modes[0] == "edge":
                    s = jnp.where(col > row, s, NEG_INF)
            else:
                parts = []
                for u, md in enumerate(modes):
                    su = s_ref[r0:r1, u * c:(u + 1) * c]
                    if has_bias:
                        su = su + (far_ref[hb] if md == "far" else bt_ref[ht, 0 if md == "diag" else 1])
                    if md == "diag":
                        su = jnp.where(row + (h * gr if kind == "mla" else 0) >= col, su, NEG_INF)
                    parts.append(su)
                s = _rep_cat(parts)
            mx = jnp.max(s, axis=-1, keepdims=True)
            if shift is not None:
                mx = mx + shift
            m_prev = m_ref[r0:r1]
            m_new = jnp.maximum(m_prev, mx)
            alpha = jnp.exp(m_prev - m_new)
            sub = m_new if shift is None else m_new - shift
            p = jnp.exp(s - _rep(sub, w // LANES))
            if modes[0] == "edge":
                p = jnp.where(col > row, p, 0.0)
            pb = p.astype(BF16)
            pv = _dot(pb[:, :sw], v_ext[:sw])
            for u in range(1, nsub):
                pv = pv + _dot(pb[:, u * sw:(u + 1) * sw], v_ext[u * sw:(u + 1) * sw])
            acc_ref[r0:r1] = _rep(alpha, acc_ref.shape[1] // LANES) * acc_ref[r0:r1] + pv
            m_ref[r0:r1] = m_new

    if kind == "win":
        nback = WINDOW // c
        assert nback * c == WINDOW and nback >= 2
        for back in range(nback, 0, -1):
            mode = "edge" if back == nback else ("near" if back == 1 else "far")

            @pl.when(qi - back >= 0)
            def _(back=back, mode=mode):
                qk((qi - back) * c, c)
                consume((qi - back) * c, (mode,))
        qk(qi * c, c)
        consume(qi * c, ("diag",))
    else:
        n_far = jnp.maximum(qi - 1, 0)
        lead = (n_far % 2) * c

        @pl.when(n_far % 2 == 1)
        def _():
            qk(0, c)
            consume(0, ("far",))

        qk(lead, 2 * c)

        def far_body(j, carry):
            consume(lead + j * (2 * c), ("far", "far"))
            qk(lead + (j + 1) * (2 * c), 2 * c)
            return carry
        lax.fori_loop(0, n_far // 2, far_body, 0)

        @pl.when(qi >= 1)
        def _():
            consume((qi - 1) * c, ("near", "diag"))

        @pl.when(qi == 0)
        def _():
            consume(0, ("diag",))

    def out_stream(h):
        r0, r1 = h * gr, (h + 1) * gr
        return acc_ref[r0:r1, :dv] / jnp.maximum(acc_ref[r0:r1, dv:], 1e-30)

    if kind == "slc":
        for h in range(ns):
            o_ref[:, h * dv:(h + 1) * dv] = out_stream(h).astype(o_ref.dtype)
    elif kind == "win":
        gate = jax.nn.sigmoid(gl_ref[...])
        for h in range(ns):
            sl = slice(h * dv, (h + 1) * dv)
            o = (gate[:, 3 * h:3 * h + 1] * ocmp_ref[:, sl]
                 + gate[:, 3 * h + 1:3 * h + 2] * oslc_ref[:, sl]
                 + gate[:, 3 * h + 2:3 * h + 3] * out_stream(h))
            o_ref[:, sl] = o.astype(o_ref.dtype)
    elif kind == "diff":
        o = out_stream(0) - lam_ref[0] * out_stream(1)
        y = o * lax.rsqrt(jnp.mean(o * o, axis=-1, keepdims=True) + RMS_EPS)
        o_ref[...] = (y * g2_ref[...]).astype(o_ref.dtype)
    else:
        for h in range(ns):
            o_ref[h * gr:(h + 1) * gr, :] = out_stream(h).astype(o_ref.dtype)


def _attn_call(kind, scalars, tensors, in_specs, out_spec, out_shape, grid, *, c, ns, gr, dk, dv):
    scratch = []
    if kind != "mla":
        scratch.append(pltpu.VMEM((ns * gr, dk), BF16))
    assert dv == LANES
    sw_cols = c if kind == "win" else 2 * c
    scratch += [pltpu.VMEM((ns * gr, LANES), F32), pltpu.VMEM((ns * gr, dv + LANES), F32),
                pltpu.VMEM((ns * gr, sw_cols), F32)]
    assert kind == "win" or grid[2] >= 2
    cfg = dict(kind=kind, c=c, ns=ns)
    return pl.pallas_call(
        functools.partial(_attn_kernel, cfg=cfg),
        grid_spec=pltpu.PrefetchScalarGridSpec(
            num_scalar_prefetch=len(scalars), grid=grid, in_specs=in_specs,
            out_specs=out_spec, scratch_shapes=scratch),
        out_shape=out_shape,
        compiler_params=_cparams(3),
        name="attn_" + kind,
    )(*scalars, *tensors)


def nsa_slc(p3, k_aug, sel, tiles, far, *, c=BIG_C):
    b, t, _ = p3.shape
    c = min(c, t // 2)
    g, hpg, dh = NSA_KV_HEADS, NSA_HPG, HEAD_DIM
    vb = 14
    in_specs = [pl.BlockSpec((None, c, hpg * dh), lambda bi, gi, qi, *_: (bi, qi, gi)),
                pl.BlockSpec((None, t, dh + LANES), lambda bi, gi, qi, *_: (bi, 0, gi)),
                pl.BlockSpec((None, t, dh), lambda bi, gi, qi, *_: (bi, 0, vb + gi)),
                pl.BlockSpec((hpg, 2, c, c), lambda bi, gi, qi, *_: (gi, 0, 0, 0)),
                pl.BlockSpec((None, None, c, LANES), lambda bi, gi, qi, *_: (bi, gi, qi, 0))]
    return _attn_call(
        "slc", [far], [p3, k_aug, p3, tiles, sel], in_specs,
        pl.BlockSpec((None, c, hpg * dh), lambda bi, gi, qi, *_: (bi, qi, gi)),
        jax.ShapeDtypeStruct((b, t, NSA_HEADS * dh), F32), (b, g, t // c),
        c=c, ns=hpg, gr=c, dk=dh + LANES, dv=dh)


def nsa_win(p3, gates3, o_cmp, o_slc, tiles, far, *, c=ATT_C):
    b, t, _ = p3.shape
    g, hpg, dh = NSA_KV_HEADS, NSA_HPG, HEAD_DIM
    kb, vb = 16, 18
    ospec = pl.BlockSpec((None, c, hpg * dh), lambda bi, gi, qi, *_: (bi, qi, gi))
    in_specs = [ospec,
                pl.BlockSpec((None, t, dh), lambda bi, gi, qi, *_: (bi, 0, kb + gi)),
                pl.BlockSpec((None, t, dh), lambda bi, gi, qi, *_: (bi, 0, vb + gi)),
                pl.BlockSpec((hpg, 2, c, c), lambda bi, gi, qi, *_: (gi, 0, 0, 0)),
                pl.BlockSpec((None, c, LANES), lambda bi, gi, qi, *_: (bi, qi, gi)),
                ospec, ospec]
    return _attn_call(
        "win", [far], [p3, p3, p3, tiles, gates3, o_cmp, o_slc], in_specs, ospec,
        jax.ShapeDtypeStruct((b, t, NSA_HEADS * dh), BF16), (b, g, t // c),
        c=c, ns=hpg, gr=c, dk=dh, dv=dh)


def diff_attn(p3, tiles, far, lam, g2, *, c=BIG_C):
    b, t, _ = p3.shape
    c = min(c, t // 2)
    dh = HEAD_DIM
    qb, kb, vb = 20, 28, 36
    in_specs = [pl.BlockSpec((None, c, dh), lambda bi, hi, qi, *_: (bi, qi, qb + hi)),
                pl.BlockSpec((None, t, dh), lambda bi, hi, qi, *_: (bi, 0, kb + hi)),
                pl.BlockSpec((None, t, dh), lambda bi, hi, qi, *_: (bi, 0, vb + hi)),
                pl.BlockSpec((1, 2, c, c), lambda bi, hi, qi, *_: (hi, 0, 0, 0)),
                pl.BlockSpec((1, dh), lambda bi, hi, qi, *_: (0, 0))]
    return _attn_call(
        "diff", [far, lam], [p3, p3, p3, tiles, g2], in_specs,
        pl.BlockSpec((None, c, dh), lambda bi, hi, qi, *_: (bi, qi, hi)),
        jax.ShapeDtypeStruct((b, t, DIFF_HEADS * dh), BF16), (b, DIFF_HEADS, t // c),
        c=c, ns=2, gr=c, dk=dh, dv=dh)


def mla_attn(qf, kf, kv3, *, c=MLA_C):
    b, t, _ = qf.shape
    dk = 2 * HEAD_DIM
    c = min(c, t // 2)
    in_specs = [pl.BlockSpec((None, c, dk), lambda bi, hi, qi: (bi, qi, hi)),
                pl.BlockSpec((None, t, dk), lambda bi, hi, qi: (bi, 0, hi)),
                pl.BlockSpec((None, t, MLA_V), lambda bi, hi, qi: (bi, 0, MLA_HEADS + hi))]
    return _attn_call(
        "mla", [], [qf, kf, kv3], in_specs,
        pl.BlockSpec((None, c, MLA_V), lambda bi, hi, qi: (bi, qi, hi)),
        jax.ShapeDtypeStruct((b, t, MLA_HEADS * MLA_V), BF16), (b, MLA_HEADS, t // c),
        c=c, ns=2, gr=c // 2, dk=dk, dv=MLA_V)


def _mla_assemble_kernel(q_ref, kv_ref, c_ref, cos_ref, sin_ref, qf_ref, kf_ref):
    h16 = MLA_HEADS
    cos = cos_ref[...]
    sin = sin_ref[...]
    lane = lax.broadcasted_iota(I32, cos.shape, 1)
    first = lane < MLA_ROPE
    kpe = c_ref[:, 0:LANES] * cos + c_ref[:, LANES:2 * LANES] * sin
    for j in range(h16 // 2):
        pe = q_ref[:, (h16 + j) * LANES:(h16 + j + 1) * LANES]
        pep = q_ref[:, (h16 + h16 // 2 + j) * LANES:(h16 + h16 // 2 + j + 1) * LANES]
        qpe = pe * cos + pep * sin
        for par in range(2):
            h = 2 * j + par
            keep = first if par == 0 else jnp.logical_not(first)
            qf_ref[:, (2 * h) * LANES:(2 * h + 1) * LANES] = q_ref[:, h * LANES:(h + 1) * LANES].astype(BF16)
            qf_ref[:, (2 * h + 1) * LANES:(2 * h + 2) * LANES] = jnp.where(keep, qpe, 0.0).astype(BF16)
            kf_ref[:, (2 * h) * LANES:(2 * h + 1) * LANES] = kv_ref[:, h * LANES:(h + 1) * LANES].astype(BF16)
            kf_ref[:, (2 * h + 1) * LANES:(2 * h + 2) * LANES] = jnp.where(keep, kpe, 0.0).astype(BF16)


def mla_assemble(qraw, kvraw, craw, cos2, sin2, seq, *, tm=256):
    n = qraw.shape[0]
    tm = min(tm, seq)
    nt = seq // tm
    wq = qraw.shape[1]
    return pl.pallas_call(
        _mla_assemble_kernel,
        grid=(n // tm,),
        in_specs=[pl.BlockSpec((tm, wq), lambda i: (i, 0)),
                  pl.BlockSpec((tm, MLA_HEADS * LANES), lambda i: (i, 0)),
                  pl.BlockSpec((tm, 2 * LANES), lambda i: (i, 4)),
                  pl.BlockSpec((tm, LANES), lambda i: (i % nt, 0)),
                  pl.BlockSpec((tm, LANES), lambda i: (i % nt, 0))],
        out_specs=[pl.BlockSpec((tm, 2 * MLA_HEADS * LANES), lambda i: (i, 0)),
                   pl.BlockSpec((tm, 2 * MLA_HEADS * LANES), lambda i: (i, 0))],
        out_shape=[jax.ShapeDtypeStruct((n, 2 * MLA_HEADS * LANES), BF16)] * 2,
        compiler_params=_cparams(1),
        name="mla_assemble",
    )(qraw, kvraw, craw, cos2, sin2)


def _mem_attn_kernel(q_ref, kv_ref, o_ref):
    hd = MEM_HEAD_DIM
    voff = MEM_HEADS * hd
    for h in range(MEM_HEADS):
        sl = slice(h * hd, (h + 1) * hd)
        s = _dot_nt(q_ref[:, sl], kv_ref[:, sl])
        m = jnp.max(s, axis=-1, keepdims=True)
        p = jnp.exp(s - m)
        p = p / jnp.sum(p, axis=-1, keepdims=True)
        o_ref[:, sl] = _dot(p.astype(BF16), kv_ref[:, voff + h * hd:voff + (h + 1) * hd]).astype(o_ref.dtype)


def mem_attn(q3, kv3, *, tq=512):
    b, t, w = q3.shape
    m = kv3.shape[1]
    tq = min(tq, t)
    return pl.pallas_call(
        _mem_attn_kernel,
        grid=(b, t // tq),
        in_specs=[pl.BlockSpec((None, tq, w), lambda bi, qi: (bi, qi, 0)),
                  pl.BlockSpec((None, m, 2 * w), lambda bi, qi: (bi, 0, 0))],
        out_specs=pl.BlockSpec((None, tq, w), lambda bi, qi: (bi, qi, 0)),
        out_shape=jax.ShapeDtypeStruct((b, t, w), BF16),
        compiler_params=_cparams(2),
        name="mem_attn",
    )(q3, kv3)


def _router_kernel(h_ref, g_ref, rw_ref, rb_ref, hn_ref, ro_ref):
    x = h_ref[...]
    hn = x * lax.rsqrt(jnp.mean(x * x, axis=-1, keepdims=True) + RMS_EPS) * g_ref[...]
    hn_ref[...] = _pack_halves(hn)
    logits = jnp.dot(hn, rw_ref[...], precision=HIGHEST, preferred_element_type=F32) + rb_ref[...]
    lane = lax.broadcasted_iota(I32, logits.shape, 1)
    lg = jnp.where(lane < N_EXPERTS, logits, -jnp.inf)
    v1 = jnp.max(lg, axis=-1, keepdims=True)
    i1 = jnp.min(jnp.where(lg == v1, lane, LANES), axis=-1, keepdims=True)
    lg2 = jnp.where(lane == i1, -jnp.inf, lg)
    v2 = jnp.max(lg2, axis=-1, keepdims=True)
    i2 = jnp.min(jnp.where(lg2 == v2, lane, LANES), axis=-1, keepdims=True)
    e2 = jnp.exp(v2 - v1)
    w1 = 1.0 / (1.0 + e2)
    w2 = e2 / (1.0 + e2)
    ro_ref[...] = jnp.where(lane == 0, i1.astype(F32),
                            jnp.where(lane == 1, i2.astype(F32),
                                      jnp.where(lane == 2, w1, jnp.where(lane == 3, w2, 0.0))))


def moe_router(h, g, rw, rb, *, tm=256):
    n, d = h.shape
    rwp = jnp.zeros((d, LANES), F32).at[:, :N_EXPERTS].set(rw.astype(F32))
    rbp = jnp.zeros((1, LANES), F32).at[0, :N_EXPERTS].set(rb.astype(F32))
    return pl.pallas_call(
        _router_kernel,
        grid=(n // tm,),
        in_specs=[pl.BlockSpec((tm, d), lambda i: (i, 0)),
                  pl.BlockSpec((1, d), lambda i: (0, 0)),
                  pl.BlockSpec((d, LANES), lambda i: (0, 0)),
                  pl.BlockSpec((1, LANES), lambda i: (0, 0))],
        out_specs=[pl.BlockSpec((tm, d // 2), lambda i: (i, 0)),
                   pl.BlockSpec((tm, LANES), lambda i: (i, 0))],
        out_shape=[jax.ShapeDtypeStruct((n, d // 2), jnp.uint32), jax.ShapeDtypeStruct((n, LANES), F32)],
        compiler_params=_cparams(1),
        name="moe_router",
    )(h, g.reshape(1, d).astype(F32), rwp, rbp)


def _row_copy(src, dst, sem):
    return pltpu.make_async_copy(src, dst, sem)


def _dispatch_kernel(d1_ref, d2_ref, x_ref, xs_in_ref, xs_ref, sem):
    del xs_in_ref
    tm = x_ref.shape[0]
    base = pl.program_id(0) * tm

    def issue(r, carry):
        src = x_ref.at[pl.ds(r, 1), :]
        _row_copy(src, xs_ref.at[pl.ds(d1_ref[base + r], 1), :], sem).start(priority=0)
        _row_copy(src, xs_ref.at[pl.ds(d2_ref[base + r], 1), :], sem).start(priority=1)
        return carry

    def drain(r, carry):
        src = x_ref.at[pl.ds(r, 1), :]
        _row_copy(src, xs_ref.at[pl.ds(0, 1), :], sem).wait()
        _row_copy(src, xs_ref.at[pl.ds(0, 1), :], sem).wait()
        return carry

    lax.fori_loop(0, tm, issue, 0)
    lax.fori_loop(0, tm, drain, 0)


def moe_dispatch(hn, d1, d2, rows, *, tm=256):
    n, d = hn.shape
    return pl.pallas_call(
        _dispatch_kernel,
        grid_spec=pltpu.PrefetchScalarGridSpec(
            num_scalar_prefetch=2, grid=(n // tm,),
            in_specs=[pl.BlockSpec((tm, d), lambda i, *_: (i, 0)),
                      pl.BlockSpec(memory_space=pl.ANY)],
            out_specs=pl.BlockSpec(memory_space=pl.ANY),
            scratch_shapes=[pltpu.SemaphoreType.DMA(())]),
        out_shape=jax.ShapeDtypeStruct((rows, d), hn.dtype),
        input_output_aliases={3: 0},
        compiler_params=_cparams(1),
        name="moe_dispatch",
    )(d1, d2, hn, jnp.zeros((rows, d), hn.dtype))


def _combine_kernel(d1_ref, d2_ref, ys_ref, h_ref, ro_ref, o_ref, b1_ref, b2_ref, sem):
    tm = h_ref.shape[0]
    base = pl.program_id(0) * tm

    def issue(r, carry):
        _row_copy(ys_ref.at[pl.ds(d1_ref[base + r], 1), :], b1_ref.at[pl.ds(r, 1), :], sem).start(priority=0)
        _row_copy(ys_ref.at[pl.ds(d2_ref[base + r], 1), :], b2_ref.at[pl.ds(r, 1), :], sem).start(priority=1)
        return carry

    def drain(r, carry):
        _row_copy(ys_ref.at[pl.ds(0, 1), :], b1_ref.at[pl.ds(r, 1), :], sem).wait()
        _row_copy(ys_ref.at[pl.ds(0, 1), :], b2_ref.at[pl.ds(r, 1), :], sem).wait()
        return carry

    lax.fori_loop(0, tm, issue, 0)
    lax.fori_loop(0, tm, drain, 0)
    ro = ro_ref[...]
    o_ref[...] = h_ref[...] + ro[:, 2:3] * b1_ref[...] + ro[:, 3:4] * b2_ref[...]


def moe_combine(ys, h, ro, d1, d2, *, tm=256):
    n, d = h.shape
    return pl.pallas_call(
        _combine_kernel,
        grid_spec=pltpu.PrefetchScalarGridSpec(
            num_scalar_prefetch=2, grid=(n // tm,),
            in_specs=[pl.BlockSpec(memory_space=pl.ANY),
                      pl.BlockSpec((tm, d), lambda i, *_: (i, 0)),
                      pl.BlockSpec((tm, LANES), lambda i, *_: (i, 0))],
            out_specs=pl.BlockSpec((tm, d), lambda i, *_: (i, 0)),
            scratch_shapes=[pltpu.VMEM((tm, d), F32), pltpu.VMEM((tm, d), F32),
                            pltpu.SemaphoreType.DMA(())]),
        out_shape=jax.ShapeDtypeStruct((n, d), F32),
        compiler_params=_cparams(1),
        name="moe_combine",
    )(d1, d2, ys, h, ro)


def moe_block(h, g, rw, rb, w_gu, w_down, layer, *, tm_e=EXPERT_TM):
    n, d = h.shape
    tm_e = min(tm_e, n)
    hn, ro = moe_router(h, g, rw, rb)
    e1 = ro[:, 0].astype(I32)
    e2 = ro[:, 1].astype(I32)
    ind = (jax.nn.one_hot(e1, N_EXPERTS, dtype=I32) + jax.nn.one_hot(e2, N_EXPERTS, dtype=I32))
    csum = jnp.cumsum(ind, axis=0)
    rank = csum - ind
    cnt = csum[-1]
    pcnt = (cnt + tm_e - 1) // tm_e * tm_e
    ends = jnp.cumsum(pcnt)
    off = ends - pcnt
    d1 = (off[e1] + jnp.take_along_axis(rank, e1[:, None], axis=1)[:, 0]).astype(I32)
    d2 = (off[e2] + jnp.take_along_axis(rank, e2[:, None], axis=1)[:, 0]).astype(I32)
    rows = 2 * n + N_EXPERTS * tm_e
    tile_start = jnp.arange(rows // tm_e, dtype=I32) * tm_e
    tile_e = jnp.minimum(jnp.sum((tile_start[:, None] >= ends[None, :]).astype(I32), axis=1),
                         N_EXPERTS - 1).astype(I32)
    xs = moe_dispatch(hn, d1, d2, rows)
    tile_w = tile_e + layer * N_EXPERTS
    n_valid = (ends[-1] // tm_e).astype(I32).reshape(1)
    act = gmm(xs, w_gu, tile_w, n_valid, tm=tm_e, out_dtype=BF16, swiglu=True, packed_x=True)
    ys = gmm(act, w_down, tile_w, n_valid, tm=tm_e, out_dtype=F32)
    return moe_combine(ys, h, ro, d1, d2)


def _mem_block(h, mem2, b, t, i, g_ca, g_mem, wq, wkv, wo):
    n, d = h.shape
    memn = rmsnorm(mem2, g_mem[i], BF16)
    q = mm(h, (wq[i] * MEM_HEAD_DIM ** -0.5).astype(BF16), out_dtype=BF16, norm_g=g_ca[i], tm=1024)
    kv = mm(memn, wkv, group=i, out_dtype=BF16)
    o = mem_attn(q.reshape(b, t, -1), kv.reshape(b, mem2.shape[0] // b, -1))
    return mm(o.reshape(n, -1), wo, group=i, out_dtype=F32, res=h)


def _even_layer(h, b, t, li, lam_init, p, tabs):
    n, d = h.shape
    w_in = p["even_w_in"][li]
    nq, nkv = NSA_HEADS * HEAD_DIM, NSA_KV_HEADS * HEAD_DIM
    g0 = nq + 6 * nkv
    dq0 = g0 + 3 * NSA_HEADS
    dqw = DIFF_HEADS * 2 * DIFF_QK_DIM
    w_main = jnp.concatenate([w_in[:, :nq] * HEAD_DIM ** -0.5, w_in[:, nq:g0],
                              w_in[:, dq0:dq0 + dqw] * DIFF_QK_DIM ** -0.5, w_in[:, dq0 + dqw:]],
                             axis=1).astype(BF16)
    w_gate = jnp.zeros((d, NSA_KV_HEADS * LANES), F32)
    for g in range(NSA_KV_HEADS):
        w_gate = w_gate.at[:, g * LANES:g * LANES + 3 * NSA_HPG].set(
            w_in[:, g0 + g * 3 * NSA_HPG:g0 + (g + 1) * 3 * NSA_HPG])
    xn = rmsnorm(h, p["norm_mix"][2 * li], BF16)
    proj = mm(xn, w_main, out_dtype=BF16)
    gates = mm(xn, w_gate.astype(BF16), out_dtype=F32, tm=1024)
    p3 = proj.reshape(b, t, -1)

    kvs = p3[:, :, nq:nq + 2 * nkv].reshape(b, t, 2, NSA_KV_HEADS, HEAD_DIM)
    acat = kvs.transpose(2, 0, 3, 1, 4).reshape(2, b, NSA_KV_HEADS, t // CMP_STRIDE, CMP_STRIDE * HEAD_DIM)
    w1 = p["nsa_cmp_w1"][li]
    half = CMP_STRIDE * HEAD_DIM
    w1cat = jnp.concatenate([w1[:, :half], w1[:, half:]], axis=2).astype(BF16)
    pos = p["nsa_cmp_pos"][li].reshape(2, 2, half)
    pos2 = jnp.zeros((2, 8, half), F32).at[:, :2].set(pos).astype(BF16)
    kvc = nsa_compress(acat, w1cat, p["nsa_cmp_w2"][li].astype(BF16), pos2)
    o_cmp, sel = nsa_cmp_select(p3, kvc, tabs["cbias"], tabs["overlap"])
    k_slc = p3[:, :, nq + 2 * nkv:nq + 3 * nkv].reshape(b, t, NSA_KV_HEADS, HEAD_DIM)
    k_aug = jnp.concatenate([k_slc, jnp.broadcast_to(tabs["blk_onehot"][None, :, None, :],
                                                     (b, t, NSA_KV_HEADS, LANES))], axis=-1)
    o_slc = nsa_slc(p3, k_aug.reshape(b, t, -1), sel, tabs["tiles_a2"], tabs["far_a"])
    o_a = nsa_win(p3, gates.reshape(b, t, -1), o_cmp, o_slc, tabs["tiles_a"], tabs["far_a"])

    lq1, lk1, lq2, lk2 = p["diff_lambda"][li].astype(F32)
    lam = jnp.exp(jnp.sum(lq1 * lk1)) - jnp.exp(jnp.sum(lq2 * lk2)) + lam_init
    g2 = (p["diff_subln"][li].astype(F32) * (1.0 - lam_init)).reshape(1, HEAD_DIM)
    o_b = diff_attn(p3, tabs["tiles_b"], tabs["far_b"], lam.reshape(1), g2)

    mix_in = jnp.concatenate([o_a, o_b], axis=-1).reshape(n, -1)
    h = mm(mix_in, p["even_w_out"], group=li, out_dtype=F32, res=h)
    return h


def _odd_layer(h, b, t, li, p, tabs):
    n, d = h.shape
    hh, nope, rp = MLA_HEADS, MLA_NOPE, MLA_ROPE
    scale = (nope + rp) ** -0.5

    def rot(w):
        w = w.reshape(w.shape[0], -1, 2, rp // 2)
        return jnp.stack([-w[:, :, 1], w[:, :, 0]], axis=2).reshape(w.shape[0], -1)

    w_in = p["odd_w_in"][li]
    kpe_w = w_in[:, MLA_Q_RANK + MLA_KV_RANK:]
    w_in_ext = jnp.concatenate([w_in[:, :MLA_Q_RANK + MLA_KV_RANK], kpe_w, kpe_w, rot(kpe_w), rot(kpe_w)],
                               axis=1).astype(BF16)
    wq = (p["mla_w_q_up"][li] * scale).reshape(MLA_Q_RANK, hh, nope + rp)
    wq_pe = wq[:, :, nope:].reshape(MLA_Q_RANK, hh * rp)
    wq_ext = jnp.concatenate([wq[:, :, :nope].reshape(MLA_Q_RANK, hh * nope), wq_pe, rot(wq_pe)],
                             axis=1).astype(BF16)
    wkv = p["mla_w_kv_up"][li].reshape(MLA_KV_RANK, hh, nope + MLA_V)
    wkv_ext = jnp.concatenate([wkv[:, :, :nope].reshape(MLA_KV_RANK, -1),
                               wkv[:, :, nope:].reshape(MLA_KV_RANK, -1)], axis=1).astype(BF16)

    craw = mm(h, w_in_ext, out_dtype=F32, norm_g=p["norm_mix"][2 * li + 1], tm=1024)
    cqn = rmsnorm(craw, p["mla_q_norm"][li], BF16, width=MLA_Q_RANK, col_block=0)
    ckvn = rmsnorm(craw, p["mla_kv_norm"][li], BF16, width=MLA_KV_RANK, col_block=1)
    qraw = mm(cqn, wq_ext, out_dtype=F32)
    kvraw = mm(ckvn, wkv_ext, out_dtype=BF16, tm=1024)
    qf, kf = mla_assemble(qraw, kvraw, craw, tabs["cos2"], tabs["sin2"], t)
    o = mla_attn(qf.reshape(b, t, -1), kf.reshape(b, t, -1), kvraw.reshape(b, t, -1))
    return mm(o.reshape(n, -1), p["odd_w_out"], group=li, out_dtype=F32, res=h)


def _tables(rel_bias, t):
    bias_a, bias_b = rel_bias[:, :NSA_HEADS], rel_bias[:, NSA_HEADS:]
    tiles_a, far_a = _bias_tiles(bias_a, ATT_C)
    tiles_a2, _ = _bias_tiles(bias_a, min(BIG_C, t // 2))
    tiles_b, far_b = _bias_tiles(bias_b, min(BIG_C, t // 2))
    ncb = t // CMP_STRIDE
    nslc = t // SLC_BLOCK
    tt = jnp.arange(t)
    blk_start = jnp.arange(ncb) * CMP_STRIDE
    cbias = _bias_lookup(tt[:, None] - (blk_start + CMP_BLOCK - 1)[None, :], bias_a)
    jj = jnp.arange(nslc)
    n_cmp = (t - CMP_BLOCK) // CMP_STRIDE + 1
    overlap = ((blk_start[:, None] < (jj[None, :] + 1) * SLC_BLOCK)
               & (blk_start[:, None] + CMP_BLOCK > jj[None, :] * SLC_BLOCK)
               & (jnp.arange(ncb)[:, None] < n_cmp)).astype(F32)
    blk_onehot = (jnp.arange(LANES)[None, :] == (tt[:, None] // SLC_BLOCK)).astype(BF16)
    inv_freq = ROPE_THETA ** (-jnp.arange(0, MLA_ROPE, 2, dtype=F32) / MLA_ROPE)
    ang = jnp.arange(t, dtype=F32)[:, None] * inv_freq
    cos2 = jnp.tile(jnp.cos(ang), (1, 2 * LANES // MLA_ROPE))
    sin2 = jnp.tile(jnp.sin(ang), (1, 2 * LANES // MLA_ROPE))
    return dict(tiles_a=tiles_a, tiles_a2=tiles_a2, far_a=far_a, tiles_b=tiles_b, far_b=far_b, cbias=cbias,
                overlap=overlap.T, blk_onehot=blk_onehot, cos2=cos2, sin2=sin2)


def kernel(x, mem, rel_bias, norm_mix, norm_mem, norm_ca, norm_ffn, norm_final, even_w_in, even_w_out, nsa_cmp_pos, nsa_cmp_w1, nsa_cmp_w2, diff_lambda, diff_subln, ffn_w_gu, ffn_w_down, odd_w_in, mla_q_norm, mla_kv_norm, mla_w_q_up, mla_w_kv_up, odd_w_out, router_w, router_b, moe_w_gu, moe_w_down, ca_wq, ca_wkv, ca_wo):
    p = dict(norm_mix=norm_mix, even_w_in=even_w_in, even_w_out=even_w_out, nsa_cmp_pos=nsa_cmp_pos,
             nsa_cmp_w1=nsa_cmp_w1, nsa_cmp_w2=nsa_cmp_w2, diff_lambda=diff_lambda, diff_subln=diff_subln,
             odd_w_in=odd_w_in, mla_q_norm=mla_q_norm, mla_kv_norm=mla_kv_norm, mla_w_q_up=mla_w_q_up,
             mla_w_kv_up=mla_w_kv_up, odd_w_out=odd_w_out)
    b, t, d = x.shape
    depth = norm_mix.shape[0]
    tabs = _tables(rel_bias, t)
    h = x.reshape(b * t, d).astype(F32)
    mem2 = mem.reshape(-1, d).astype(F32)
    for i in range(depth):
        li = i // 2
        if i % 2 == 0:
            h = _even_layer(h, b, t, li, 0.8 - 0.6 * math.exp(-0.3 * i), p, tabs)
        else:
            h = _odd_layer(h, b, t, li, p, tabs)
        h = _mem_block(h, mem2, b, t, i, norm_ca, norm_mem, ca_wq, ca_wkv, ca_wo)
        if i % 2 == 0:
            hn = rmsnorm(h, norm_ffn[i], BF16)
            act = mm(hn, ffn_w_gu, group=li, out_dtype=BF16, swiglu=True, tm=1024)
            h = mm(act, ffn_w_down, group=li, out_dtype=F32, res=h)
        else:
            h = moe_block(h, norm_ffn[i], router_w[li], router_b[li],
                          moe_w_gu.reshape((-1,) + moe_w_gu.shape[2:]),
                          moe_w_down.reshape((-1,) + moe_w_down.shape[2:]), li)
    return rmsnorm(h, norm_final, x.dtype).reshape(b, t, d)
```
